```python
import jax, jax.numpy as jnp
from jax import lax
import numpy as np

D_MODEL = 1024
BATCH = 8
SEQ = 4096
DEPTH = 2

GRID_W = 64
CTX_LEN = 256
N_MOD = 6
EPS = 1e-6
N_EVEN = (DEPTH + 1) // 2
N_ODD = DEPTH // 2
D_CONV = 512
CONV_WIDTH = 31
CONV_GROUPS = 8
NA_HEADS = 8
HEAD_DIM = 64
D_ATTN = NA_HEADS * HEAD_DIM
NA_ROWS_MAX = 8
NA_COLS = 16
NA_COL_BLOCK = 16
NA_BAND = NA_COL_BLOCK + NA_COLS
D_AB_IN = 2 * D_CONV + 3 * D_ATTN
CHUNK = 128
SGU_GROUPS = 8
D_SGU = 1024
PEER_HEADS = 8
N_KEYS = 128
N_EXPERTS = N_KEYS * N_KEYS
PEER_TOPK = 16
D_KEY = 128
PEER_BLOCK = 128

kernel_name = 'hybrid_conv_natten_sgu_peer_dit'


def rms_norm(x, g):
    xf = x.astype(jnp.float32)
    y = xf * lax.rsqrt(jnp.mean(xf * xf, -1, keepdims=True) + EPS)
    return y.astype(x.dtype) * g


def standardize(x, groups):
    xf = x.astype(jnp.float32).reshape(x.shape[:-1] + (groups, -1))
    mu = jnp.mean(xf, -1, keepdims=True)
    var = jnp.mean(jnp.square(xf - mu), -1, keepdims=True)
    return ((xf - mu) * lax.rsqrt(var + EPS)).reshape(x.shape).astype(x.dtype)


def ada_modulation(cvec, w, b):
    m = jax.nn.silu(cvec) @ w + b
    return m.reshape(m.shape[:-1] + (N_MOD, D_MODEL))


def modulate(x, g, shift, scale):
    return rms_norm(x, g) * (1 + scale[:, None]) + shift[:, None]


def conformer_conv(u, conv_w, conv_b, gn_g, gn_b):
    a, gate = jnp.split(u, 2, axis=-1)
    h = a * jax.nn.sigmoid(gate)
    h = lax.conv_general_dilated(
        h, conv_w[:, None, :].astype(h.dtype), window_strides=(1,),
        padding=[(CONV_WIDTH // 2, CONV_WIDTH // 2)],
        dimension_numbers=('NWC', 'WIO', 'NWC'), feature_group_count=D_CONV) + conv_b
    h = standardize(h, CONV_GROUPS) * gn_g + gn_b
    return jax.nn.silu(h)


def neighbourhood_attention(q, k, v, k_c, v_c, rpb):
    B, S, H, Dh = q.shape
    rows = S // GRID_W
    wh = min(NA_ROWS_MAX, rows)
    ncb = GRID_W // NA_COL_BLOCK
    r = jnp.arange(rows)
    key_rows = jnp.clip(r - wh // 2, 0, rows - wh)[:, None] + jnp.arange(wh)
    band_start = jnp.clip(jnp.arange(ncb) * NA_COL_BLOCK - NA_COLS // 2, 0, GRID_W - NA_BAND)
    key_cols = band_start[:, None] + jnp.arange(NA_BAND)
    q_cols = jnp.arange(GRID_W).reshape(ncb, NA_COL_BLOCK)
    col_start = jnp.clip(q_cols - NA_COLS // 2, 0, GRID_W - NA_COLS)
    valid = (key_cols[:, None, :] >= col_start[..., None]) & (key_cols[:, None, :] < col_start[..., None] + NA_COLS)
    dr_idx = key_rows - r[:, None] + (NA_ROWS_MAX - 1)
    dc_idx = jnp.clip(key_cols[:, None, :] - q_cols[..., None] + (NA_COLS - 1), 0, 2 * NA_COLS - 2)
    bias = rpb.astype(jnp.float32)[:, dr_idx[:, None, None, :, None], dc_idx[None, :, :, None, :]]
    bias = jnp.moveaxis(bias, 0, 2)
    bias = jnp.where(valid[None, :, None, :, None, :], bias, jnp.float32(-1e30))
    ridx = key_rows[:, None, :, None]
    cidx = key_cols[None, :, None, :]
    scale = HEAD_DIM ** -0.5
    nwin = wh * NA_BAND

    def one_sample(args):
        qs, ks, vs, kcs, vcs = args
        qb = qs.reshape(rows, ncb, NA_COL_BLOCK, H, Dh)
        kg = ks.reshape(rows, GRID_W, H, Dh)[ridx, cidx]
        vg = vs.reshape(rows, GRID_W, H, Dh)[ridx, cidx]
        s_win = jnp.einsum('rjqhd,rjwbhd->rjhqwb', qb, kg).astype(jnp.float32) * scale + bias
        s_ctx = jnp.einsum('rjqhd,chd->rjhqc', qb, kcs).astype(jnp.float32) * scale
        s = jnp.concatenate([s_win.reshape(rows, ncb, H, NA_COL_BLOCK, nwin), s_ctx], axis=-1)
        p = jax.nn.softmax(s, axis=-1).astype(vs.dtype)
        p_win = p[..., :nwin].reshape(rows, ncb, H, NA_COL_BLOCK, wh, NA_BAND)
        o = (jnp.einsum('rjhqwb,rjwbhd->rjqhd', p_win, vg)
             + jnp.einsum('rjhqc,chd->rjqhd', p[..., nwin:], vcs))
        return o.reshape(S, H * Dh)

    return lax.map(one_sample, (q, k, v, k_c, v_c))


def context_attention(q, k, v):
    B, L, H, Dh = q.shape
    s = jnp.einsum('bqhd,bkhd->bhqk', q, k).astype(jnp.float32) * HEAD_DIM ** -0.5
    p = jax.nn.softmax(s, axis=-1).astype(v.dtype)
    return jnp.einsum('bhqk,bkhd->bqhd', p, v).reshape(B, L, H * Dh)


def heads(t):
    return t.reshape(t.shape[:-1] + (NA_HEADS, HEAD_DIM))


def mixer_conv_na(hm, hcm, w_in, conv_w, conv_b, gn_g, gn_b, rpb, w_out, ctx_out):
    proj = hm @ w_in
    q, k, v = jnp.split(proj[..., 2 * D_CONV:], 3, axis=-1)
    kv_c = hcm @ w_in[:, 2 * D_CONV + D_ATTN:]
    k_c, v_c = jnp.split(kv_c, 2, axis=-1)
    y_a = conformer_conv(proj[..., :2 * D_CONV], conv_w, conv_b, gn_g, gn_b)
    y_b = neighbourhood_attention(heads(q), heads(k), heads(v), heads(k_c), heads(v_c), rpb)
    out = jnp.concatenate([y_a, y_b], axis=-1) @ w_out
    if not ctx_out:
        return out, None
    proj_c = hcm @ w_in[:, :2 * D_CONV + D_ATTN]
    yc_a = conformer_conv(proj_c[..., :2 * D_CONV], conv_w, conv_b, gn_g, gn_b)
    yc_b = context_attention(heads(proj_c[..., 2 * D_CONV:]), heads(k_c), heads(v_c))
    return out, jnp.concatenate([yc_a, yc_b], axis=-1) @ w_out


def mixer_sgu(hm, w_in, ln_g, ln_b, w_s, b_s, w_out):
    B, L, _ = hm.shape
    z = jax.nn.gelu(hm @ w_in, approximate=False)
    u, vv = jnp.split(z, 2, axis=-1)
    vv = standardize(vv, 1) * ln_g + ln_b
    vv = vv.reshape(B, L // CHUNK, CHUNK, SGU_GROUPS, D_SGU // SGU_GROUPS)
    mixed = jnp.einsum('gpq,bnqgc->bnpgc', w_s, vv) + b_s.T[None, None, :, :, None]
    return (u * mixed.reshape(B, L, D_SGU)) @ w_out


def peer(h, wq, keys, u_tab, v_tab):
    B, L, D = h.shape
    blocks = h.reshape(-1, PEER_BLOCK, D)

    def one_block(xb):
        T = xb.shape[0]
        q = (xb @ wq).reshape(T, PEER_HEADS, 2, D_KEY)
        s = jnp.einsum('thpd,hpkd->thpk', q, keys).astype(jnp.float32)
        s1, i1 = lax.top_k(s[:, :, 0], PEER_TOPK)
        s2, i2 = lax.top_k(s[:, :, 1], PEER_TOPK)
        cand = (s1[..., :, None] + s2[..., None, :]).reshape(T, PEER_HEADS, PEER_TOPK * PEER_TOPK)
        cidx = (i1[..., :, None] * N_KEYS + i2[..., None, :]).reshape(T, PEER_HEADS, PEER_TOPK * PEER_TOPK)
        top_s, pos = lax.top_k(cand, PEER_TOPK)
        idx = jnp.take_along_axis(cidx, pos, axis=-1)
        g = jax.nn.softmax(top_s, axis=-1).astype(xb.dtype)
        ue = u_tab[idx]
        ve = v_tab[idx]
        act = jax.nn.gelu(jnp.einsum('td,thkd->thk', xb, ue), approximate=False)
        return jnp.einsum('thk,thkd->td', g * act, ve)

    return lax.map(one_block, blocks).reshape(B, L, D)


def setup_inputs(seed: int = 0) -> dict:
    key = jax.random.key(seed)
    ks = iter(jax.random.split(key, 32))
    D = D_MODEL

    def nrm(shape, s):
        return jax.random.normal(next(ks), shape, jnp.float32) * s

    return {
        'x': nrm((BATCH, SEQ, D), 1.0),
        'c': nrm((BATCH, D), 1.0),
        'ctx': nrm((BATCH, CTX_LEN, D), 1.0),
        'c_ctx': nrm((D,), 1.0),
        'norm1_g': 1.0 + nrm((DEPTH, D), 0.02),
        'norm2_g': 1.0 + nrm((DEPTH, D), 0.02),
        'ada_w': nrm((DEPTH, D, N_MOD * D), 0.5 * D ** -0.5),
        'ada_b': nrm((DEPTH, N_MOD * D), 0.02),
        'ab_w_in': nrm((N_EVEN, D, D_AB_IN), D ** -0.5),
        'conv_w': nrm((N_EVEN, CONV_WIDTH, D_CONV), CONV_WIDTH ** -0.5),
        'conv_b': nrm((N_EVEN, D_CONV), 0.02),
        'conv_gn_g': 1.0 + nrm((N_EVEN, D_CONV), 0.02),
        'conv_gn_b': nrm((N_EVEN, D_CONV), 0.02),
        'na_rpb': nrm((N_EVEN, NA_HEADS, 2 * NA_ROWS_MAX - 1, 2 * NA_COLS - 1), 0.5),
        'ab_w_out': nrm((N_EVEN, D_CONV + D_ATTN, D), (D_CONV + D_ATTN) ** -0.5),
        'sgu_w_in': nrm((N_ODD, D, 2 * D_SGU), D ** -0.5),
        'sgu_ln_g': 1.0 + nrm((N_ODD, D_SGU), 0.02),
        'sgu_ln_b': nrm((N_ODD, D_SGU), 0.02),
        'sgu_w_s': nrm((N_ODD, SGU_GROUPS, CHUNK, CHUNK), CHUNK ** -0.5),
        'sgu_b_s': 1.0 + nrm((N_ODD, SGU_GROUPS, CHUNK), 0.02),
        'sgu_w_out': nrm((N_ODD, D_SGU, D), D_SGU ** -0.5),
        'peer_wq': nrm((DEPTH, D, PEER_HEADS * 2 * D_KEY), D ** -0.5),
        'peer_keys': nrm((DEPTH, PEER_HEADS, 2, N_KEYS, D_KEY), D_KEY ** -0.5),
        'peer_u': nrm((DEPTH, N_EXPERTS, D), D ** -0.5),
        'peer_v': nrm((DEPTH, N_EXPERTS, D), 1.0),
        'norm_f_g': 1.0 + nrm((D,), 0.02),
    }


def reference(x, c, ctx, c_ctx, norm1_g, norm2_g, ada_w, ada_b, ab_w_in, conv_w, conv_b,
              conv_gn_g, conv_gn_b, na_rpb, ab_w_out, sgu_w_in, sgu_ln_g, sgu_ln_b, sgu_w_s,
              sgu_b_s, sgu_w_out, peer_wq, peer_keys, peer_u, peer_v, norm_f_g):
    h, hc = x, ctx
    for i in range(DEPTH):
        ctx_carried = any(j % 2 == 0 for j in range(i + 1, DEPTH))
        mod = ada_modulation(c, ada_w[i], ada_b[i])
        hm = modulate(h, norm1_g[i], mod[:, 0], mod[:, 1])
        if i % 2 == 0 or ctx_carried:
            mod_c = ada_modulation(c_ctx[None], ada_w[i], ada_b[i])
            hcm = modulate(hc, norm1_g[i], mod_c[:, 0], mod_c[:, 1])
        if i % 2 == 0:
            e = i // 2
            mix, mix_c = mixer_conv_na(hm, hcm, ab_w_in[e], conv_w[e], conv_b[e], conv_gn_g[e],
                                       conv_gn_b[e], na_rpb[e], ab_w_out[e], ctx_carried)
        else:
            o = i // 2
            sgu_args = (sgu_w_in[o], sgu_ln_g[o], sgu_ln_b[o], sgu_w_s[o], sgu_b_s[o], sgu_w_out[o])
            mix = mixer_sgu(hm, *sgu_args)
            mix_c = mixer_sgu(hcm, *sgu_args) if ctx_carried else None
        peer_args = (peer_wq[i], peer_keys[i], peer_u[i], peer_v[i])
        h = h + mod[:, 2, None] * mix
        h = h + mod[:, 5, None] * peer(modulate(h, norm2_g[i], mod[:, 3], mod[:, 4]), *peer_args)
        if ctx_carried:
            hc = hc + mod_c[:, 2, None] * mix_c
            hc = hc + mod_c[:, 5, None] * peer(modulate(hc, norm2_g[i], mod_c[:, 3], mod_c[:, 4]), *peer_args)
    return rms_norm(h, norm_f_g)
```

```python
import functools

import jax
import jax.numpy as jnp
from jax import lax
from jax.experimental import pallas as pl
from jax.experimental.pallas import tpu as pltpu

F32 = jnp.float32
BF16 = jnp.bfloat16
I32 = jnp.int32

D_MODEL = 1024
BATCH = 8
SEQ = 4096
DEPTH = 2
GRID_W = 64
GRID_H = SEQ // GRID_W
CTX_LEN = 256
N_MOD = 6
EPS = 1e-6
D_CONV = 512
CONV_WIDTH = 31
CONV_HALF = CONV_WIDTH // 2
CONV_GROUPS = 8
NA_HEADS = 8
HEAD_DIM = 64
D_ATTN = NA_HEADS * HEAD_DIM
NA_ROWS = 8
NA_COLS = 16
D_AB_IN = 2 * D_CONV + 3 * D_ATTN
CHUNK = 128
SGU_GROUPS = 8
D_SGU = 1024
PEER_HEADS = 8
N_KEYS = 128
N_EXPERTS = N_KEYS * N_KEYS
PEER_TOPK = 16
D_KEY = 128
N_SLOTS = PEER_HEADS * PEER_TOPK

N_TOK = BATCH * SEQ

SUBLANES = 8
LANES = 128
VMEM_LIMIT = 56 * 1024 * 1024

NEG_BIG = -1e30


def _cparams(n_axes, vmem=VMEM_LIMIT):
    return pltpu.CompilerParams(dimension_semantics=("parallel",) * n_axes, vmem_limit_bytes=vmem)


def _rms_modulate(xf, g, shift, scale):
    y = xf * lax.rsqrt(jnp.mean(xf * xf, axis=-1, keepdims=True) + EPS)
    return (y * g) * (1.0 + scale) + shift


def _gelu_exact(x):
    return 0.5 * x * (1.0 + lax.erf(x * (2.0 ** -0.5)))


def _silu(x):
    return x * jax.nn.sigmoid(x)


ADA_TN = 1536


def _ada_kernel(c_ref, w_ref, b_ref, o_ref):
    s = _silu(c_ref[...])
    o_ref[0] = jnp.dot(s.astype(BF16), w_ref[0].astype(BF16), preferred_element_type=F32) + b_ref[0]


def _ada_modulation(cc, ada_w, ada_b):
    n = N_MOD * D_MODEL
    return pl.pallas_call(
        _ada_kernel,
        grid=(DEPTH, n // ADA_TN),
        in_specs=[
            pl.BlockSpec((16, D_MODEL), lambda l, j: (0, 0)),
            pl.BlockSpec((1, D_MODEL, ADA_TN), lambda l, j: (l, 0, j)),
            pl.BlockSpec((1, 1, ADA_TN), lambda l, j: (l, 0, j)),
        ],
        out_specs=pl.BlockSpec((1, 16, ADA_TN), lambda l, j: (l, 0, j)),
        out_shape=jax.ShapeDtypeStruct((DEPTH, 16, n), F32),
        compiler_params=_cparams(2),
        name="ada_modulation",
    )(cc, ada_w, ada_b.reshape(DEPTH, 1, n))


TM_PROJ = 512


def _inproj0_kernel(x_ref, mod_ref, g_ref, w_ref, hglu_ref, q_ref, k_ref, v_ref):
    hm = _rms_modulate(x_ref[...], g_ref[...], mod_ref[0, 0:1, :], mod_ref[0, 1:2, :])
    proj = jnp.dot(hm.astype(BF16), w_ref[...], preferred_element_type=F32)
    a = proj[:, :D_CONV]
    gate = proj[:, D_CONV:2 * D_CONV]
    hglu_ref[...] = a * jax.nn.sigmoid(gate)
    o = 2 * D_CONV
    q_ref[...] = proj[:, o:o + D_ATTN].astype(BF16)
    k_ref[...] = proj[:, o + D_ATTN:o + 2 * D_ATTN].astype(BF16)
    v_ref[...] = proj[:, o + 2 * D_ATTN:o + 3 * D_ATTN].astype(BF16)


def _inproj0(x2d, mod, g, w_in_bf):
    tiles_per_b = SEQ // TM_PROJ
    tok = lambda i: (i, 0)
    return pl.pallas_call(
        _inproj0_kernel,
        grid=(N_TOK // TM_PROJ,),
        in_specs=[
            pl.BlockSpec((TM_PROJ, D_MODEL), tok),
            pl.BlockSpec((1, N_MOD, D_MODEL), lambda i: (i // tiles_per_b, 0, 0)),
            pl.BlockSpec((1, D_MODEL), lambda i: (0, 0)),
            pl.BlockSpec((D_MODEL, D_AB_IN), lambda i: (0, 0)),
        ],
        out_specs=[
            pl.BlockSpec((TM_PROJ, D_CONV), tok),
            pl.BlockSpec((TM_PROJ, D_ATTN), tok),
            pl.BlockSpec((TM_PROJ, D_ATTN), tok),
            pl.BlockSpec((TM_PROJ, D_ATTN), tok),
        ],
        out_shape=[
            jax.ShapeDtypeStruct((N_TOK, D_CONV), F32),
            jax.ShapeDtypeStruct((N_TOK, D_ATTN), BF16),
            jax.ShapeDtypeStruct((N_TOK, D_ATTN), BF16),
            jax.ShapeDtypeStruct((N_TOK, D_ATTN), BF16),
        ],
        compiler_params=_cparams(1),
        name="inproj0",
    )(x2d, mod, g, w_in_bf)


def _ctx_kv_kernel(x_ref, mod_ref, g_ref, w_ref, k_ref, v_ref):
    hm = _rms_modulate(x_ref[...], g_ref[...], mod_ref[0:1, :], mod_ref[1:2, :])
    proj = jnp.dot(hm.astype(BF16), w_ref[...], preferred_element_type=F32)
    k_ref[...] = proj[:, :D_ATTN].astype(BF16)
    v_ref[...] = proj[:, D_ATTN:].astype(BF16)


def _ctx_kv(ctx2d, mod_c, g, w_kv_bf):
    n = ctx2d.shape[0]
    tm = 512
    tok = lambda i: (i, 0)
    return pl.pallas_call(
        _ctx_kv_kernel,
        grid=(n // tm,),
        in_specs=[
            pl.BlockSpec((tm, D_MODEL), tok),
            pl.BlockSpec((N_MOD, D_MODEL), lambda i: (0, 0)),
            pl.BlockSpec((1, D_MODEL), lambda i: (0, 0)),
            pl.BlockSpec((D_MODEL, 2 * D_ATTN), lambda i: (0, 0)),
        ],
        out_specs=[pl.BlockSpec((tm, D_ATTN), tok), pl.BlockSpec((tm, D_ATTN), tok)],
        out_shape=[jax.ShapeDtypeStruct((n, D_ATTN), BF16), jax.ShapeDtypeStruct((n, D_ATTN), BF16)],
        compiler_params=_cparams(1),
        name="ctx_kv",
    )(ctx2d, mod_c, g, w_kv_bf)


TL_CONV = 256
CONV_SUB = 64
CONV_PAD = 16


def _split_dot(x, a_bf):
    hi = x.astype(BF16)
    lo = (x - hi.astype(F32)).astype(BF16)
    return (jnp.dot(hi, a_bf, preferred_element_type=F32) + jnp.dot(lo, a_bf, preferred_element_type=F32))


def _conv_kernel(prev_ref, cur_ref, next_ref, cw_ref, cb_ref, gg_ref, gb_ref, avg_ref, o_ref, win_ref):
    j = pl.program_id(1)
    nj = pl.num_programs(1)
    zeros = jnp.zeros((CONV_PAD, D_CONV), F32)
    win_ref[0:CONV_PAD, :] = jnp.where(j > 0, prev_ref[0, TL_CONV - CONV_PAD:TL_CONV, :], zeros)
    win_ref[CONV_PAD:CONV_PAD + TL_CONV, :] = cur_ref[0]
    win_ref[CONV_PAD + TL_CONV:2 * CONV_PAD + TL_CONV, :] = jnp.where(j < nj - 1, next_ref[0, 0:CONV_PAD, :], zeros)
    avg = avg_ref[...]
    for r0 in range(0, TL_CONV, CONV_SUB):
        acc = jnp.zeros((CONV_SUB, D_CONV), F32) + cb_ref[...]
        for k in range(CONV_WIDTH):
            off = r0 + CONV_PAD - CONV_HALF + k
            acc = acc + win_ref[off:off + CONV_SUB, :] * cw_ref[k:k + 1, :]
        mu = _split_dot(acc, avg)
        d = acc - mu
        var = _split_dot(d * d, avg)
        y = d * lax.rsqrt(var + EPS) * gg_ref[...] + gb_ref[...]
        o_ref[0, r0:r0 + CONV_SUB, :] = _silu(y).astype(BF16)


def _conv_module(hglu3, cw, cb, gg, gb):
    nj = SEQ // TL_CONV
    gsz = D_CONV // CONV_GROUPS
    gid = jnp.arange(D_CONV) // gsz
    avg = jnp.where(gid[:, None] == gid[None, :], 1.0 / gsz, 0.0).astype(BF16)
    vec = pl.BlockSpec((1, D_CONV), lambda b, j: (0, 0))
    return pl.pallas_call(
        _conv_kernel,
        grid=(BATCH, nj),
        in_specs=[
            pl.BlockSpec((1, TL_CONV, D_CONV), lambda b, j: (b, jnp.maximum(j - 1, 0), 0)),
            pl.BlockSpec((1, TL_CONV, D_CONV), lambda b, j: (b, j, 0)),
            pl.BlockSpec((1, TL_CONV, D_CONV), lambda b, j: (b, jnp.minimum(j + 1, nj - 1), 0)),
            pl.BlockSpec((CONV_WIDTH, D_CONV), lambda b, j: (0, 0)),
            vec, vec, vec,
            pl.BlockSpec((D_CONV, D_CONV), lambda b, j: (0, 0)),
        ],
        out_specs=pl.BlockSpec((1, TL_CONV, D_CONV), lambda b, j: (b, j, 0)),
        out_shape=jax.ShapeDtypeStruct((BATCH, SEQ, D_CONV), BF16),
        scratch_shapes=[pltpu.VMEM((TL_CONV + 2 * CONV_PAD, D_CONV), F32)],
        compiler_params=_cparams(2),
        name="conv_module",
    )(hglu3, hglu3, hglu3, cw, cb.reshape(1, D_CONV), gg.reshape(1, D_CONV), gb.reshape(1, D_CONV), avg)


NA_WIN = NA_ROWS * GRID_W


def _na_kernel(q_ref, k_ref, v_ref, kc_ref, vc_ref, bias_ref, o_ref):
    r = pl.program_id(1)
    kr0 = jnp.clip(r - NA_ROWS // 2, 0, GRID_H - NA_ROWS)
    start = pl.multiple_of(kr0 * GRID_W, GRID_W)
    scale = HEAD_DIM ** -0.5
    lane = lax.broadcasted_iota(I32, (GRID_W, LANES), 1)
    for hp in range(NA_HEADS // 2):
        ls = slice(hp * LANES, (hp + 1) * LANES)
        qp = q_ref[:, ls]
        kw = k_ref[0, pl.ds(start, NA_WIN), ls]
        vw = v_ref[0, pl.ds(start, NA_WIN), ls]
        kc = kc_ref[0, :, ls]
        vc = vc_ref[0, :, ls]
        outs = []
        for hh in range(2):
            in_head = (lane >= hh * HEAD_DIM) & (lane < (hh + 1) * HEAD_DIM)
            qm = jnp.where(in_head, qp, jnp.zeros_like(qp))
            dn = (((1,), (1,)), ((), ()))
            s_win = lax.dot_general(qm, kw, dn, preferred_element_type=F32) * scale + bias_ref[0, 2 * hp + hh]
            s_ctx = lax.dot_general(qm, kc, dn, preferred_element_type=F32) * scale
            m = jnp.maximum(jnp.max(s_win, axis=-1, keepdims=True), jnp.max(s_ctx, axis=-1, keepdims=True))
            p_win = jnp.exp(s_win - m)
            p_ctx = jnp.exp(s_ctx - m)
            inv = 1.0 / (jnp.sum(p_win, axis=-1, keepdims=True) + jnp.sum(p_ctx, axis=-1, keepdims=True))
            o = (jnp.dot((p_win * inv).astype(BF16), vw, preferred_element_type=F32)
                 + jnp.dot((p_ctx * inv).astype(BF16), vc, preferred_element_type=F32))
            outs.append(o)
        o_ref[:, ls] = jnp.where(lane < HEAD_DIM, outs[0], outs[1]).astype(BF16)


def _na_bias_table(rpb):
    qc = jnp.arange(GRID_W)
    kc = jnp.arange(GRID_W)
    col_start = jnp.clip(qc - NA_COLS // 2, 0, GRID_W - NA_COLS)
    valid = (kc[None, :] >= col_start[:, None]) & (kc[None, :] < col_start[:, None] + NA_COLS)
    dc = jnp.clip(kc[None, :] - qc[:, None] + (NA_COLS - 1), 0, 2 * NA_COLS - 2)
    dr = jnp.arange(NA_ROWS)[:, None] + jnp.arange(NA_ROWS)[None, :]
    tab = rpb.astype(F32)[:, dr[:, :, None, None], dc[None, None, :, :]]
    tab = jnp.where(valid[None, None, None], tab, F32(NEG_BIG))
    tab = jnp.transpose(tab, (1, 0, 3, 2, 4))
    return tab.reshape(NA_ROWS, NA_HEADS, GRID_W, NA_WIN)


def _neighbourhood_attention(q, k3, v3, kc3, vc3, bias_tab):
    def bias_map(b, r):
        kr0 = jnp.clip(r - NA_ROWS // 2, 0, GRID_H - NA_ROWS)
        return (kr0 - r + (NA_ROWS - 1), 0, 0, 0)

    return pl.pallas_call(
        _na_kernel,
        grid=(BATCH, GRID_H),
        in_specs=[
            pl.BlockSpec((GRID_W, D_ATTN), lambda b, r: (b * GRID_H + r, 0)),
            pl.BlockSpec((1, SEQ, D_ATTN), lambda b, r: (b, 0, 0)),
            pl.BlockSpec((1, SEQ, D_ATTN), lambda b, r: (b, 0, 0)),
            pl.BlockSpec((1, CTX_LEN, D_ATTN), lambda b, r: (b, 0, 0)),
            pl.BlockSpec((1, CTX_LEN, D_ATTN), lambda b, r: (b, 0, 0)),
            pl.BlockSpec((1, NA_HEADS, GRID_W, NA_WIN), bias_map),
        ],
        out_specs=pl.BlockSpec((GRID_W, D_ATTN), lambda b, r: (b * GRID_H + r, 0)),
        out_shape=jax.ShapeDtypeStruct((N_TOK, D_ATTN), BF16),
        compiler_params=_cparams(2),
        name="natten",
    )(q, k3, v3, kc3, vc3, bias_tab)


TM_OUT = 512


def _outproj0_kernel(ya_ref, yb_ref, h_ref, mod_ref, g2_ref, wa_ref, wb_ref, h1_ref, x2_ref):
    mix = (jnp.dot(ya_ref[...], wa_ref[...], preferred_element_type=F32)
           + jnp.dot(yb_ref[...], wb_ref[...], preferred_element_type=F32))
    h1 = h_ref[...] + mod_ref[0, 2:3, :] * mix
    h1_ref[...] = h1
    x2_ref[...] = _rms_modulate(h1, g2_ref[...], mod_ref[0, 3:4, :], mod_ref[0, 4:5, :])


def _outproj0(ya, yb, h, mod, g2, wa_bf, wb_bf):
    tiles_per_b = SEQ // TM_OUT
    tok = lambda i: (i, 0)
    full = lambda i: (0, 0)
    return pl.pallas_call(
        _outproj0_kernel,
        grid=(N_TOK // TM_OUT,),
        in_specs=[
            pl.BlockSpec((TM_OUT, D_CONV), tok),
            pl.BlockSpec((TM_OUT, D_ATTN), tok),
            pl.BlockSpec((TM_OUT, D_MODEL), tok),
            pl.BlockSpec((1, N_MOD, D_MODEL), lambda i: (i // tiles_per_b, 0, 0)),
            pl.BlockSpec((1, D_MODEL), full),
            pl.BlockSpec((D_CONV, D_MODEL), full),
            pl.BlockSpec((D_ATTN, D_MODEL), full),
        ],
        out_specs=[pl.BlockSpec((TM_OUT, D_MODEL), tok), pl.BlockSpec((TM_OUT, D_MODEL), tok)],
        out_shape=[jax.ShapeDtypeStruct((N_TOK, D_MODEL), F32), jax.ShapeDtypeStruct((N_TOK, D_MODEL), F32)],
        compiler_params=_cparams(1),
        name="outproj0",
    )(ya, yb, h, mod, g2, wa_bf, wb_bf)


TM_ROUTE = 256


def _topk_cols(s, vals_of, k):
    n, tm = s.shape
    rows = lax.broadcasted_iota(I32, (n, tm), 0).astype(F32)
    krow = lax.broadcasted_iota(I32, (k, tm), 0)
    top = jnp.zeros((k, tm), F32)
    pay = None
    for it in range(k):
        m = jnp.max(s, axis=0, keepdims=True)
        pos = jnp.min(jnp.where(s == m, rows, F32(n)), axis=0, keepdims=True)
        sel = rows == pos
        extra = vals_of(sel, pos)
        if pay is None:
            pay = [jnp.zeros((k, tm), F32) for _ in extra]
        top = jnp.where(krow == it, m, top)
        pay = [jnp.where(krow == it, e, p) for e, p in zip(extra, pay)]
        s = jnp.where(sel, -jnp.inf, s)
    return top, pay


def _route_kernel(x_ref, wq_ref, keys_ref, idx_ref, gate_ref):
    q = jnp.dot(x_ref[...].astype(BF16), wq_ref[...], preferred_element_type=F32).astype(BF16)
    dn = (((1,), (1,)), ((), ()))
    for h in range(PEER_HEADS):
        halves = []
        for p in range(2):
            c0 = (2 * h + p) * D_KEY
            st = lax.dot_general(keys_ref[h, p], q[:, c0:c0 + D_KEY], dn, preferred_element_type=F32)
            top, (ki,) = _topk_cols(st, lambda sel, pos: [pos], PEER_TOPK)
            halves.append((top, ki))
        (s1, i1), (s2, i2) = halves
        cand = jnp.concatenate([s1[a:a + 1, :] + s2 for a in range(PEER_TOPK)], axis=0)
        cidx = jnp.concatenate([i1[a:a + 1, :] * F32(N_KEYS) + i2 for a in range(PEER_TOPK)], axis=0)
        top_s, (eidx,) = _topk_cols(
            cand, lambda sel, pos: [jnp.max(jnp.where(sel, cidx, -1.0), axis=0, keepdims=True)], PEER_TOPK)
        e = jnp.exp(top_s - top_s[0:1, :])
        gate = e / jnp.sum(e, axis=0, keepdims=True)
        idx_ref[h * PEER_TOPK:(h + 1) * PEER_TOPK, :] = eidx.astype(I32)
        gate_ref[h * PEER_TOPK:(h + 1) * PEER_TOPK, :] = gate


def _peer_route(x2, wq_bf, keys_bf):
    col = lambda i: (0, i)
    return pl.pallas_call(
        _route_kernel,
        grid=(N_TOK // TM_ROUTE,),
        in_specs=[
            pl.BlockSpec((TM_ROUTE, D_MODEL), lambda i: (i, 0)),
            pl.BlockSpec((D_MODEL, PEER_HEADS * 2 * D_KEY), lambda i: (0, 0)),
            pl.BlockSpec((PEER_HEADS, 2, N_KEYS, D_KEY), lambda i: (0, 0, 0, 0)),
        ],
        out_specs=[pl.BlockSpec((N_SLOTS, TM_ROUTE), col), pl.BlockSpec((N_SLOTS, TM_ROUTE), col)],
        out_shape=[jax.ShapeDtypeStruct((N_SLOTS, N_TOK), I32), jax.ShapeDtypeStruct((N_SLOTS, N_TOK), F32)],
        compiler_params=_cparams(1),
        name="peer_route",
    )(x2, wq_bf, keys_bf)


N_TILES = N_EXPERTS // 2
HALF_ROWS = SUBLANES // 2
TB_PEER = 128
SLOT_GROUP = 8


def _pack_expert_table(tab):
    bits = lax.bitcast_convert_type(tab.astype(BF16), jnp.uint16).astype(jnp.uint32)
    half = D_MODEL // 2
    packed = (bits[:, half:] << 16) | bits[:, :half]
    return lax.bitcast_convert_type(packed, I32).reshape(N_TILES, SUBLANES, LANES)


def _unpack_tile(w):
    hi = pltpu.bitcast(w & I32(-65536), F32)
    lo = pltpu.bitcast(w << 16, F32)
    return hi, lo


def _peer_dots_kernel(tile_ref, x_ref, tbl_ref, out_ref):
    sub = lax.broadcasted_iota(I32, (SUBLANES, LANES), 0)
    lane = lax.broadcasted_iota(I32, (SUBLANES, LANES), 1)
    low_half = sub < HALF_ROWS
    m2 = (sub & 2) == 0
    m1 = (sub & 1) == 0
    n_acc = 2 * (N_SLOTS // SLOT_GROUP)
    order = (0, 2, 1, 3, 4, 6, 5, 7)

    def token(t, accs):
        x = x_ref[t]
        xr = pltpu.roll(x, HALF_ROWS, 0)
        x_lo = jnp.where(low_half, x, xr)
        x_hi = jnp.where(low_half, xr, x)
        new = []
        for g in range(N_SLOTS // SLOT_GROUP):
            ps = []
            for n in order:
                hi, lo = _unpack_tile(tbl_ref[tile_ref[t, g * SLOT_GROUP + n]])
                ps.append(hi * x_hi + lo * x_lo)
            st = [jnp.where(m2, ps[a] + pltpu.roll(ps[a], 6, 0), ps[a + 1] + pltpu.roll(ps[a + 1], 2, 0))
                  for a in range(0, SLOT_GROUP, 2)]
            qs = [jnp.where(m1, st[a] + pltpu.roll(st[a], 7, 0), st[a + 1] + pltpu.roll(st[a + 1], 1, 0))
                  for a in range(0, SLOT_GROUP // 2, 2)]
            for j, qv in enumerate(qs):
                r = jnp.sum(qv, axis=1, keepdims=True)
                new.append(jnp.where(lane == t, r, accs[2 * g + j]))
        return tuple(new)

    accs = lax.fori_loop(0, TB_PEER, token, tuple(jnp.zeros((SUBLANES, LANES), F32) for _ in range(n_acc)))
    for i, a in enumerate(accs):
        out_ref[i * SUBLANES:(i + 1) * SUBLANES, :] = a


def _peer_dots(tile_tok, x3, table):
    return pl.pallas_call(
        _peer_dots_kernel,
        grid=(N_TOK // TB_PEER,),
        in_specs=[
            pl.BlockSpec((TB_PEER, N_SLOTS), lambda i: (i, 0), memory_space=pltpu.SMEM),
            pl.BlockSpec((TB_PEER, SUBLANES, LANES), lambda i: (i, 0, 0)),
            pl.BlockSpec((N_TILES, SUBLANES, LANES), lambda i: (0, 0, 0), pipeline_mode=pl.Buffered(1)),
        ],
        out_specs=pl.BlockSpec((2 * N_SLOTS, TB_PEER), lambda i: (0, i)),
        out_shape=jax.ShapeDtypeStruct((2 * N_SLOTS, N_TOK), F32),
        compiler_params=_cparams(1),
        name="peer_dots",
    )(tile_tok, x3, table)


TM_ACT = 512


def _peer_act_kernel(da_ref, db_ref, idx_ref, gate_ref, wa_ref, wb_ref):
    odd = (idx_ref[...] & 1) == 1
    w = gate_ref[...] * _gelu_exact(jnp.where(odd, db_ref[...], da_ref[...]))
    zero = jnp.zeros_like(w)
    wa_ref[...] = jnp.where(odd, zero, w).T
    wb_ref[...] = jnp.where(odd, w, zero).T


def _peer_act(dots_a, dots_b, idx_t, gate_t):
    col = pl.BlockSpec((N_SLOTS, TM_ACT), lambda i: (0, i))
    row = pl.BlockSpec((TM_ACT, N_SLOTS), lambda i: (i, 0))
    return pl.pallas_call(
        _peer_act_kernel,
        grid=(N_TOK // TM_ACT,),
        in_specs=[col, col, col, col],
        out_specs=[row, row],
        out_shape=[jax.ShapeDtypeStruct((N_TOK, N_SLOTS), F32), jax.ShapeDtypeStruct((N_TOK, N_SLOTS), F32)],
        compiler_params=_cparams(1),
        name="peer_act",
    )(dots_a, dots_b, idx_t, gate_t)


N_VACC = 4


def _peer_combine_kernel(tile_ref, wa_ref, wb_ref, tbl_ref, out_ref):
    sub = lax.broadcasted_iota(I32, (SUBLANES, LANES), 0)
    low_half = sub < HALF_ROWS

    def token(t, carry):
        acc_hi = [jnp.zeros((SUBLANES, LANES), F32) for _ in range(N_VACC)]
        acc_lo = [jnp.zeros((SUBLANES, LANES), F32) for _ in range(N_VACC)]
        for k in range(N_SLOTS):
            hi, lo = _unpack_tile(tbl_ref[tile_ref[t, k]])
            wv = jnp.where(low_half, wa_ref[t, k], wb_ref[t, k])
            a = k % N_VACC
            acc_hi[a] = acc_hi[a] + wv * hi
            acc_lo[a] = acc_lo[a] + wv * lo
        hi = (acc_hi[0] + acc_hi[1]) + (acc_hi[2] + acc_hi[3])
        lo = (acc_lo[0] + acc_lo[1]) + (acc_lo[2] + acc_lo[3])
        hi = hi + pltpu.roll(hi, HALF_ROWS, 0)
        lo = lo + pltpu.roll(lo, HALF_ROWS, 0)
        out_ref[t] = jnp.where(low_half, lo, hi)
        return carry

    lax.fori_loop(0, TB_PEER, token, 0)


def _peer_combine(tile_tok, wa, wb, table):
    smem = lambda: pl.BlockSpec((TB_PEER, N_SLOTS), lambda i: (i, 0), memory_space=pltpu.SMEM)
    return pl.pallas_call(
        _peer_combine_kernel,
        grid=(N_TOK // TB_PEER,),
        in_specs=[
            smem(), smem(), smem(),
            pl.BlockSpec((N_TILES, SUBLANES, LANES), lambda i: (0, 0, 0), pipeline_mode=pl.Buffered(1)),
        ],
        out_specs=pl.BlockSpec((TB_PEER, SUBLANES, LANES), lambda i: (i, 0, 0)),
        out_shape=jax.ShapeDtypeStruct((N_TOK, SUBLANES, LANES), F32),
        compiler_params=_cparams(1),
        name="peer_combine",
    )(tile_tok, wa, wb, table)


def _peer(x2, wq_bf, keys_bf, u_packed, v_packed):
    idx_t, gate_t = _peer_route(x2, wq_bf, keys_bf)
    tile_tok = (idx_t >> 1).T
    dots = _peer_dots(tile_tok, x2.reshape(N_TOK, SUBLANES, LANES), u_packed)
    d4 = dots.reshape(N_SLOTS // HALF_ROWS, 2, HALF_ROWS, N_TOK)
    wa, wb = _peer_act(d4[:, 0].reshape(N_SLOTS, N_TOK), d4[:, 1].reshape(N_SLOTS, N_TOK), idx_t, gate_t)
    out3 = _peer_combine(tile_tok, wa, wb, v_packed)
    return out3.reshape(N_TOK, D_MODEL)


TM_SGU = 256


def _sgu_kernel(h1_ref, peer_ref, mod0_ref, mod_ref, g1_ref, g2_ref, win_ref, lng_ref, lnb_ref, ws_ref, bs_ref,
                wout_ref, h3_ref, x2_ref):
    h2 = h1_ref[...] + mod0_ref[0, 5:6, :] * peer_ref[...]
    hm = _rms_modulate(h2, g1_ref[...], mod_ref[0, 0:1, :], mod_ref[0, 1:2, :])
    z = _gelu_exact(jnp.dot(hm.astype(BF16), win_ref[...], preferred_element_type=F32))
    u = z[:, :D_SGU]
    vv = z[:, D_SGU:]
    mu = jnp.mean(vv, axis=-1, keepdims=True)
    d = vv - mu
    var = jnp.mean(d * d, axis=-1, keepdims=True)
    vn = (d * lax.rsqrt(var + EPS) * lng_ref[...] + lnb_ref[...]).astype(BF16)
    gw = D_SGU // SGU_GROUPS
    rows = []
    for c in range(TM_SGU // CHUNK):
        cols = []
        for g in range(SGU_GROUPS):
            blk = vn[c * CHUNK:(c + 1) * CHUNK, g * gw:(g + 1) * gw]
            cols.append(jnp.dot(ws_ref[g], blk, preferred_element_type=F32) + bs_ref[g])
        rows.append(jnp.concatenate(cols, axis=1))
    mixed = jnp.concatenate(rows, axis=0)
    mix = jnp.dot((u * mixed).astype(BF16), wout_ref[...], preferred_element_type=F32)
    h3 = h2 + mod_ref[0, 2:3, :] * mix
    h3_ref[...] = h3
    x2_ref[...] = _rms_modulate(h3, g2_ref[...], mod_ref[0, 3:4, :], mod_ref[0, 4:5, :])


def _sgu_layer(h1, peer0, mod0, mod1, g1, g2, win_bf, lng, lnb, ws_bf, bs, wout_bf):
    tiles_per_b = SEQ // TM_SGU
    tok = lambda i: (i, 0)
    full = lambda i: (0, 0)
    modspec = pl.BlockSpec((1, N_MOD, D_MODEL), lambda i: (i // tiles_per_b, 0, 0))
    vec = pl.BlockSpec((1, D_MODEL), full)
    return pl.pallas_call(
        _sgu_kernel,
        grid=(N_TOK // TM_SGU,),
        in_specs=[
            pl.BlockSpec((TM_SGU, D_MODEL), tok),
            pl.BlockSpec((TM_SGU, D_MODEL), tok),
            modspec, modspec, vec, vec,
            pl.BlockSpec((D_MODEL, 2 * D_SGU), full),
            vec, vec,
            pl.BlockSpec((SGU_GROUPS, CHUNK, CHUNK), lambda i: (0, 0, 0)),
            pl.BlockSpec((SGU_GROUPS, CHUNK, 1), lambda i: (0, 0, 0)),
            pl.BlockSpec((D_SGU, D_MODEL), full),
        ],
        out_specs=[pl.BlockSpec((TM_SGU, D_MODEL), tok), pl.BlockSpec((TM_SGU, D_MODEL), tok)],
        out_shape=[jax.ShapeDtypeStruct((N_TOK, D_MODEL), F32), jax.ShapeDtypeStruct((N_TOK, D_MODEL), F32)],
        compiler_params=_cparams(1),
        name="sgu_layer",
    )(h1, peer0, mod0, mod1, g1, g2, win_bf, lng, lnb, ws_bf, bs, wout_bf)


TM_FINAL = 512


def _final_kernel(h_ref, peer_ref, mod_ref, g_ref, o_ref):
    h = h_ref[...] + mod_ref[0, 5:6, :] * peer_ref[...]
    o_ref[...] = (h * lax.rsqrt(jnp.mean(h * h, axis=-1, keepdims=True) + EPS)) * g_ref[...]


def _final(h3, peer1, mod1, gf):
    tiles_per_b = SEQ // TM_FINAL
    tok = lambda i: (i, 0)
    return pl.pallas_call(
        _final_kernel,
        grid=(N_TOK // TM_FINAL,),
        in_specs=[
            pl.BlockSpec((TM_FINAL, D_MODEL), tok),
            pl.BlockSpec((TM_FINAL, D_MODEL), tok),
            pl.BlockSpec((1, N_MOD, D_MODEL), lambda i: (i // tiles_per_b, 0, 0)),
            pl.BlockSpec((1, D_MODEL), lambda i: (0, 0)),
        ],
        out_specs=pl.BlockSpec((TM_FINAL, D_MODEL), tok),
        out_shape=jax.ShapeDtypeStruct((N_TOK, D_MODEL), F32),
        compiler_params=_cparams(1),
        name="final_norm",
    )(h3, peer1, mod1, gf)


def kernel(x, c, ctx, c_ctx, norm1_g, norm2_g, ada_w, ada_b, ab_w_in, conv_w, conv_b, conv_gn_g, conv_gn_b,
           na_rpb, ab_w_out, sgu_w_in, sgu_ln_g, sgu_ln_b, sgu_w_s, sgu_b_s, sgu_w_out, peer_wq, peer_keys,
           peer_u, peer_v, norm_f_g):
    assert x.shape == (BATCH, SEQ, D_MODEL) and ctx.shape == (BATCH, CTX_LEN, D_MODEL)
    x2d = x.reshape(N_TOK, D_MODEL)
    row = lambda a: a.reshape(1, -1)

    cc = jnp.concatenate([c, c_ctx[None], jnp.zeros((16 - BATCH - 1, D_MODEL), F32)], axis=0)
    mod_all = _ada_modulation(cc, ada_w, ada_b)
    mod0 = mod_all[0, :BATCH].reshape(BATCH, N_MOD, D_MODEL)
    mod0_c = mod_all[0, BATCH].reshape(N_MOD, D_MODEL)
    mod1 = mod_all[1, :BATCH].reshape(BATCH, N_MOD, D_MODEL)

    w_in = ab_w_in[0].astype(BF16)
    hglu, q, k, v = _inproj0(x2d, mod0, row(norm1_g[0]), w_in)
    kc, vc = _ctx_kv(ctx.reshape(BATCH * CTX_LEN, D_MODEL), mod0_c, row(norm1_g[0]), w_in[:, 2 * D_CONV + D_ATTN:])
    y_a = _conv_module(hglu.reshape(BATCH, SEQ, D_CONV), conv_w[0], conv_b[0], conv_gn_g[0], conv_gn_b[0])
    y_b = _neighbourhood_attention(
        q, k.reshape(BATCH, SEQ, D_ATTN), v.reshape(BATCH, SEQ, D_ATTN),
        kc.reshape(BATCH, CTX_LEN, D_ATTN), vc.reshape(BATCH, CTX_LEN, D_ATTN), _na_bias_table(na_rpb[0]))
    w_out = ab_w_out[0].astype(BF16)
    h1, x2 = _outproj0(y_a.reshape(N_TOK, D_CONV), y_b, x2d, mod0, row(norm2_g[0]), w_out[:D_CONV], w_out[D_CONV:])
    peer0 = _peer(x2, peer_wq[0].astype(BF16), peer_keys[0].astype(BF16),
                  _pack_expert_table(peer_u[0]), _pack_expert_table(peer_v[0]))

    h3, x2b = _sgu_layer(
        h1, peer0, mod0, mod1, row(norm1_g[1]), row(norm2_g[1]), sgu_w_in[0].astype(BF16), row(sgu_ln_g[0]),
        row(sgu_ln_b[0]), sgu_w_s[0].astype(BF16), sgu_b_s[0].reshape(SGU_GROUPS, CHUNK, 1),
        sgu_w_out[0].astype(BF16))
    peer1 = _peer(x2b, peer_wq[1].astype(BF16), peer_keys[1].astype(BF16),
                  _pack_expert_table(peer_u[1]), _pack_expert_table(peer_v[1]))

    out = _final(h3, peer1, mod1, row(norm_f_g))
    return out.reshape(BATCH, SEQ, D_MODEL)
```

```python
import functools

import jax
import jax.numpy as jnp
from jax import lax
from jax.experimental import pallas as pl
from jax.experimental.pallas import tpu as pltpu

F32 = jnp.float32
BF16 = jnp.bfloat16
I32 = jnp.int32

D_MODEL = 1024
BATCH = 8
SEQ = 4096
DEPTH = 2
GRID_W = 64
GRID_H = SEQ // GRID_W
CTX_LEN = 256
N_MOD = 6
EPS = 1e-6
D_CONV = 512
CONV_WIDTH = 31
CONV_HALF = CONV_WIDTH // 2
CONV_GROUPS = 8
NA_HEADS = 8
HEAD_DIM = 64
D_ATTN = NA_HEADS * HEAD_DIM
NA_ROWS = 8
NA_COLS = 16
D_AB_IN = 2 * D_CONV + 3 * D_ATTN
CHUNK = 128
SGU_GROUPS = 8
D_SGU = 1024
PEER_HEADS = 8
N_KEYS = 128
N_EXPERTS = N_KEYS * N_KEYS
PEER_TOPK = 16
D_KEY = 128
N_SLOTS = PEER_HEADS * PEER_TOPK

N_TOK = BATCH * SEQ

SUBLANES = 8
LANES = 128
VMEM_LIMIT = 56 * 1024 * 1024

NEG_BIG = -1e30


def _cparams(n_axes, vmem=VMEM_LIMIT):
    return pltpu.CompilerParams(dimension_semantics=("parallel",) * n_axes, vmem_limit_bytes=vmem)


def _rms_modulate(xf, g, shift, scale):
    y = xf * lax.rsqrt(jnp.mean(xf * xf, axis=-1, keepdims=True) + EPS)
    return (y * g) * (1.0 + scale) + shift


def _gelu_exact(x):
    return 0.5 * x * (1.0 + lax.erf(x * (2.0 ** -0.5)))


def _silu(x):
    return x * jax.nn.sigmoid(x)


ADA_TN = 1536


def _ada_kernel(c_ref, w_ref, b_ref, o_ref):
    s = _silu(c_ref[...])
    o_ref[0] = jnp.dot(s.astype(BF16), w_ref[0].astype(BF16), preferred_element_type=F32) + b_ref[0]


def _ada_modulation(cc, ada_w, ada_b):
    n = N_MOD * D_MODEL
    return pl.pallas_call(
        _ada_kernel,
        grid=(DEPTH, n // ADA_TN),
        in_specs=[
            pl.BlockSpec((16, D_MODEL), lambda l, j: (0, 0)),
            pl.BlockSpec((1, D_MODEL, ADA_TN), lambda l, j: (l, 0, j)),
            pl.BlockSpec((1, 1, ADA_TN), lambda l, j: (l, 0, j)),
        ],
        out_specs=pl.BlockSpec((1, 16, ADA_TN), lambda l, j: (l, 0, j)),
        out_shape=jax.ShapeDtypeStruct((DEPTH, 16, n), F32),
        compiler_params=_cparams(2),
        name="ada_modulation",
    )(cc, ada_w, ada_b.reshape(DEPTH, 1, n))


TM_PROJ = 512


def _inproj0_kernel(x_ref, mod_ref, g_ref, w_ref, hglu_ref, q_ref, k_ref, v_ref):
    hm = _rms_modulate(x_ref[...], g_ref[...], mod_ref[0, 0:1, :], mod_ref[0, 1:2, :])
    proj = jnp.dot(hm.astype(BF16), w_ref[...], preferred_element_type=F32)
    a = proj[:, :D_CONV]
    gate = proj[:, D_CONV:2 * D_CONV]
    hglu_ref[...] = a * jax.nn.sigmoid(gate)
    o = 2 * D_CONV
    q_ref[...] = proj[:, o:o + D_ATTN].astype(BF16)
    k_ref[...] = proj[:, o + D_ATTN:o + 2 * D_ATTN].astype(BF16)
    v_ref[...] = proj[:, o + 2 * D_ATTN:o + 3 * D_ATTN].astype(BF16)


def _inproj0(x2d, mod, g, w_in_bf):
    tiles_per_b = SEQ // TM_PROJ
    tok = lambda i: (i, 0)
    return pl.pallas_call(
        _inproj0_kernel,
        grid=(N_TOK // TM_PROJ,),
        in_specs=[
            pl.BlockSpec((TM_PROJ, D_MODEL), tok),
            pl.BlockSpec((1, N_MOD, D_MODEL), lambda i: (i // tiles_per_b, 0, 0)),
            pl.BlockSpec((1, D_MODEL), lambda i: (0, 0)),
            pl.BlockSpec((D_MODEL, D_AB_IN), lambda i: (0, 0)),
        ],
        out_specs=[
            pl.BlockSpec((TM_PROJ, D_CONV), tok),
            pl.BlockSpec((TM_PROJ, D_ATTN), tok),
            pl.BlockSpec((TM_PROJ, D_ATTN), tok),
            pl.BlockSpec((TM_PROJ, D_ATTN), tok),
        ],
        out_shape=[
            jax.ShapeDtypeStruct((N_TOK, D_CONV), F32),
            jax.ShapeDtypeStruct((N_TOK, D_ATTN), BF16),
            jax.ShapeDtypeStruct((N_TOK, D_ATTN), BF16),
            jax.ShapeDtypeStruct((N_TOK, D_ATTN), BF16),
        ],
        compiler_params=_cparams(1),
        name="inproj0",
    )(x2d, mod, g, w_in_bf)


def _ctx_kv_kernel(x_ref, mod_ref, g_ref, w_ref, k_ref, v_ref):
    hm = _rms_modulate(x_ref[...], g_ref[...], mod_ref[0:1, :], mod_ref[1:2, :])
    proj = jnp.dot(hm.astype(BF16), w_ref[...], preferred_element_type=F32)
    k_ref[...] = proj[:, :D_ATTN].astype(BF16)
    v_ref[...] = proj[:, D_ATTN:].astype(BF16)


def _ctx_kv(ctx2d, mod_c, g, w_kv_bf):
    n = ctx2d.shape[0]
    tm = 512
    tok = lambda i: (i, 0)
    return pl.pallas_call(
        _ctx_kv_kernel,
        grid=(n // tm,),
        in_specs=[
            pl.BlockSpec((tm, D_MODEL), tok),
            pl.BlockSpec((N_MOD, D_MODEL), lambda i: (0, 0)),
            pl.BlockSpec((1, D_MODEL), lambda i: (0, 0)),
            pl.BlockSpec((D_MODEL, 2 * D_ATTN), lambda i: (0, 0)),
        ],
        out_specs=[pl.BlockSpec((tm, D_ATTN), tok), pl.BlockSpec((tm, D_ATTN), tok)],
        out_shape=[jax.ShapeDtypeStruct((n, D_ATTN), BF16), jax.ShapeDtypeStruct((n, D_ATTN), BF16)],
        compiler_params=_cparams(1),
        name="ctx_kv",
    )(ctx2d, mod_c, g, w_kv_bf)


TL_CONV = 256
CONV_SUB = 64
CONV_PAD = 16


def _split_dot(x, a_bf):
    hi = x.astype(BF16)
    lo = (x - hi.astype(F32)).astype(BF16)
    return (jnp.dot(hi, a_bf, preferred_element_type=F32) + jnp.dot(lo, a_bf, preferred_element_type=F32))


def _conv_kernel(prev_ref, cur_ref, next_ref, cw_ref, cb_ref, gg_ref, gb_ref, avg_ref, o_ref, win_ref):
    j = pl.program_id(1)
    nj = pl.num_programs(1)
    zeros = jnp.zeros((CONV_PAD, D_CONV), F32)
    win_ref[0:CONV_PAD, :] = jnp.where(j > 0, prev_ref[0, TL_CONV - CONV_PAD:TL_CONV, :], zeros)
    win_ref[CONV_PAD:CONV_PAD + TL_CONV, :] = cur_ref[0]
    win_ref[CONV_PAD + TL_CONV:2 * CONV_PAD + TL_CONV, :] = jnp.where(j < nj - 1, next_ref[0, 0:CONV_PAD, :], zeros)
    avg = avg_ref[...]
    for r0 in range(0, TL_CONV, CONV_SUB):
        acc = jnp.zeros((CONV_SUB, D_CONV), F32) + cb_ref[...]
        for k in range(CONV_WIDTH):
            off = r0 + CONV_PAD - CONV_HALF + k
            acc = acc + win_ref[off:off + CONV_SUB, :] * cw_ref[k:k + 1, :]
        mu = _split_dot(acc, avg)
        d = acc - mu
        var = _split_dot(d * d, avg)
        y = d * lax.rsqrt(var + EPS) * gg_ref[...] + gb_ref[...]
        o_ref[0, r0:r0 + CONV_SUB, :] = _silu(y).astype(BF16)


def _conv_module(hglu3, cw, cb, gg, gb):
    nj = SEQ // TL_CONV
    gsz = D_CONV // CONV_GROUPS
    gid = jnp.arange(D_CONV) // gsz
    avg = jnp.where(gid[:, None] == gid[None, :], 1.0 / gsz, 0.0).astype(BF16)
    vec = pl.BlockSpec((1, D_CONV), lambda b, j: (0, 0))
    return pl.pallas_call(
        _conv_kernel,
        grid=(BATCH, nj),
        in_specs=[
            pl.BlockSpec((1, TL_CONV, D_CONV), lambda b, j: (b, jnp.maximum(j - 1, 0), 0)),
            pl.BlockSpec((1, TL_CONV, D_CONV), lambda b, j: (b, j, 0)),
            pl.BlockSpec((1, TL_CONV, D_CONV), lambda b, j: (b, jnp.minimum(j + 1, nj - 1), 0)),
            pl.BlockSpec((CONV_WIDTH, D_CONV), lambda b, j: (0, 0)),
            vec, vec, vec,
            pl.BlockSpec((D_CONV, D_CONV), lambda b, j: (0, 0)),
        ],
        out_specs=pl.BlockSpec((1, TL_CONV, D_CONV), lambda b, j: (b, j, 0)),
        out_shape=jax.ShapeDtypeStruct((BATCH, SEQ, D_CONV), BF16),
        scratch_shapes=[pltpu.VMEM((TL_CONV + 2 * CONV_PAD, D_CONV), F32)],
        compiler_params=_cparams(2),
        name="conv_module",
    )(hglu3, hglu3, hglu3, cw, cb.reshape(1, D_CONV), gg.reshape(1, D_CONV), gb.reshape(1, D_CONV), avg)


NA_WIN = NA_ROWS * GRID_W


def _na_kernel(q_ref, k_ref, v_ref, kc_ref, vc_ref, bias_ref, o_ref):
    r = pl.program_id(1)
    kr0 = jnp.clip(r - NA_ROWS // 2, 0, GRID_H - NA_ROWS)
    start = pl.multiple_of(kr0 * GRID_W, GRID_W)
    scale = HEAD_DIM ** -0.5
    lane = lax.broadcasted_iota(I32, (GRID_W, LANES), 1)
    for hp in range(NA_HEADS // 2):
        ls = slice(hp * LANES, (hp + 1) * LANES)
        qp = q_ref[:, ls]
        kw = k_ref[0, pl.ds(start, NA_WIN), ls]
        vw = v_ref[0, pl.ds(start, NA_WIN), ls]
        kc = kc_ref[0, :, ls]
        vc = vc_ref[0, :, ls]
        outs = []
        for hh in range(2):
            in_head = (lane >= hh * HEAD_DIM) & (lane < (hh + 1) * HEAD_DIM)
            qm = jnp.where(in_head, qp, jnp.zeros_like(qp))
            dn = (((1,), (1,)), ((), ()))
            s_win = lax.dot_general(qm, kw, dn, preferred_element_type=F32) * scale + bias_ref[0, 2 * hp + hh]
            s_ctx = lax.dot_general(qm, kc, dn, preferred_element_type=F32) * scale
            m = jnp.maximum(jnp.max(s_win, axis=-1, keepdims=True), jnp.max(s_ctx, axis=-1, keepdims=True))
            p_win = jnp.exp(s_win - m)
            p_ctx = jnp.exp(s_ctx - m)
            inv = 1.0 / (jnp.sum(p_win, axis=-1, keepdims=True) + jnp.sum(p_ctx, axis=-1, keepdims=True))
            o = (jnp.dot((p_win * inv).astype(BF16), vw, preferred_element_type=F32)
                 + jnp.dot((p_ctx * inv).astype(BF16), vc, preferred_element_type=F32))
            outs.append(o)
        o_ref[:, ls] = jnp.where(lane < HEAD_DIM, outs[0], outs[1]).astype(BF16)


def _na_bias_table(rpb):
    qc = jnp.arange(GRID_W)
    kc = jnp.arange(GRID_W)
    col_start = jnp.clip(qc - NA_COLS // 2, 0, GRID_W - NA_COLS)
    valid = (kc[None, :] >= col_start[:, None]) & (kc[None, :] < col_start[:, None] + NA_COLS)
    dc = jnp.clip(kc[None, :] - qc[:, None] + (NA_COLS - 1), 0, 2 * NA_COLS - 2)
    pick = (dc[:, :, None] == jnp.arange(2 * NA_COLS - 1)).astype(F32)
    by_col = jnp.einsum("hab,qkb->haqk", rpb.astype(F32), pick, precision=lax.Precision.HIGHEST)
    by_col = jnp.where(valid[None, None], by_col, F32(NEG_BIG))
    tab = jnp.stack([by_col[:, v:v + NA_ROWS] for v in range(NA_ROWS)], axis=0)
    return jnp.transpose(tab, (0, 1, 3, 2, 4)).reshape(NA_ROWS, NA_HEADS, GRID_W, NA_WIN)


def _neighbourhood_attention(q, k3, v3, kc3, vc3, bias_tab):
    def bias_map(b, r):
        kr0 = jnp.clip(r - NA_ROWS // 2, 0, GRID_H - NA_ROWS)
        return (kr0 - r + (NA_ROWS - 1), 0, 0, 0)

    return pl.pallas_call(
        _na_kernel,
        grid=(BATCH, GRID_H),
        in_specs=[
            pl.BlockSpec((GRID_W, D_ATTN), lambda b, r: (b * GRID_H + r, 0)),
            pl.BlockSpec((1, SEQ, D_ATTN), lambda b, r: (b, 0, 0)),
            pl.BlockSpec((1, SEQ, D_ATTN), lambda b, r: (b, 0, 0)),
            pl.BlockSpec((1, CTX_LEN, D_ATTN), lambda b, r: (b, 0, 0)),
            pl.BlockSpec((1, CTX_LEN, D_ATTN), lambda b, r: (b, 0, 0)),
            pl.BlockSpec((1, NA_HEADS, GRID_W, NA_WIN), bias_map),
        ],
        out_specs=pl.BlockSpec((GRID_W, D_ATTN), lambda b, r: (b * GRID_H + r, 0)),
        out_shape=jax.ShapeDtypeStruct((N_TOK, D_ATTN), BF16),
        compiler_params=_cparams(2),
        name="natten",
    )(q, k3, v3, kc3, vc3, bias_tab)


TM_OUT = 512


def _outproj0_kernel(ya_ref, yb_ref, h_ref, mod_ref, g2_ref, wa_ref, wb_ref, h1_ref, x2_ref):
    mix = (jnp.dot(ya_ref[...], wa_ref[...], preferred_element_type=F32)
           + jnp.dot(yb_ref[...], wb_ref[...], preferred_element_type=F32))
    h1 = h_ref[...] + mod_ref[0, 2:3, :] * mix
    h1_ref[...] = h1
    x2_ref[...] = _rms_modulate(h1, g2_ref[...], mod_ref[0, 3:4, :], mod_ref[0, 4:5, :])


def _outproj0(ya, yb, h, mod, g2, wa_bf, wb_bf):
    tiles_per_b = SEQ // TM_OUT
    tok = lambda i: (i, 0)
    full = lambda i: (0, 0)
    return pl.pallas_call(
        _outproj0_kernel,
        grid=(N_TOK // TM_OUT,),
        in_specs=[
            pl.BlockSpec((TM_OUT, D_CONV), tok),
            pl.BlockSpec((TM_OUT, D_ATTN), tok),
            pl.BlockSpec((TM_OUT, D_MODEL), tok),
            pl.BlockSpec((1, N_MOD, D_MODEL), lambda i: (i // tiles_per_b, 0, 0)),
            pl.BlockSpec((1, D_MODEL), full),
            pl.BlockSpec((D_CONV, D_MODEL), full),
            pl.BlockSpec((D_ATTN, D_MODEL), full),
        ],
        out_specs=[pl.BlockSpec((TM_OUT, D_MODEL), tok), pl.BlockSpec((TM_OUT, D_MODEL), tok)],
        out_shape=[jax.ShapeDtypeStruct((N_TOK, D_MODEL), F32), jax.ShapeDtypeStruct((N_TOK, D_MODEL), F32)],
        compiler_params=_cparams(1),
        name="outproj0",
    )(ya, yb, h, mod, g2, wa_bf, wb_bf)


TM_ROUTE = 256


def _topk_cols(s, vals_of, k):
    n, tm = s.shape
    rows = lax.broadcasted_iota(I32, (n, tm), 0).astype(F32)
    krow = lax.broadcasted_iota(I32, (k, tm), 0)
    top = jnp.zeros((k, tm), F32)
    pay = None
    for it in range(k):
        m = jnp.max(s, axis=0, keepdims=True)
        pos = jnp.min(jnp.where(s == m, rows, F32(n)), axis=0, keepdims=True)
        sel = rows == pos
        extra = vals_of(sel, pos)
        if pay is None:
            pay = [jnp.zeros((k, tm), F32) for _ in extra]
        top = jnp.where(krow == it, m, top)
        pay = [jnp.where(krow == it, e, p) for e, p in zip(extra, pay)]
        s = jnp.where(sel, -jnp.inf, s)
    return top, pay


def _route_kernel(x_ref, wq_ref, keys_ref, idx_ref, gate_ref):
    q = jnp.dot(x_ref[...].astype(BF16), wq_ref[...], preferred_element_type=F32).astype(BF16)
    dn = (((1,), (1,)), ((), ()))
    for h in range(PEER_HEADS):
        halves = []
        for p in range(2):
            c0 = (2 * h + p) * D_KEY
            st = lax.dot_general(keys_ref[h, p], q[:, c0:c0 + D_KEY], dn, preferred_element_type=F32)
            top, (ki,) = _topk_cols(st, lambda sel, pos: [pos], PEER_TOPK)
            halves.append((top, ki))
        (s1, i1), (s2, i2) = halves
        cand = jnp.concatenate([s1[a:a + 1, :] + s2 for a in range(PEER_TOPK)], axis=0)
        cidx = jnp.concatenate([i1[a:a + 1, :] * F32(N_KEYS) + i2 for a in range(PEER_TOPK)], axis=0)
        top_s, (eidx,) = _topk_cols(
            cand, lambda sel, pos: [jnp.max(jnp.where(sel, cidx, -1.0), axis=0, keepdims=True)], PEER_TOPK)
        e = jnp.exp(top_s - top_s[0:1, :])
        gate = e / jnp.sum(e, axis=0, keepdims=True)
        idx_ref[h * PEER_TOPK:(h + 1) * PEER_TOPK, :] = eidx.astype(I32)
        gate_ref[h * PEER_TOPK:(h + 1) * PEER_TOPK, :] = gate


def _peer_route(x2, wq_bf, keys_bf):
    col = lambda i: (0, i)
    return pl.pallas_call(
        _route_kernel,
        grid=(N_TOK // TM_ROUTE,),
        in_specs=[
            pl.BlockSpec((TM_ROUTE, D_MODEL), lambda i: (i, 0)),
            pl.BlockSpec((D_MODEL, PEER_HEADS * 2 * D_KEY), lambda i: (0, 0)),
            pl.BlockSpec((PEER_HEADS, 2, N_KEYS, D_KEY), lambda i: (0, 0, 0, 0)),
        ],
        out_specs=[pl.BlockSpec((N_SLOTS, TM_ROUTE), col), pl.BlockSpec((N_SLOTS, TM_ROUTE), col)],
        out_shape=[jax.ShapeDtypeStruct((N_SLOTS, N_TOK), I32), jax.ShapeDtypeStruct((N_SLOTS, N_TOK), F32)],
        compiler_params=_cparams(1),
        name="peer_route",
    )(x2, wq_bf, keys_bf)


N_TILES = N_EXPERTS // 2
HALF_ROWS = SUBLANES // 2
TB_PEER = 128
SLOT_GROUP = 8


def _pack_expert_table(tab):
    bits = lax.bitcast_convert_type(tab.astype(BF16), jnp.uint16).astype(jnp.uint32)
    half = D_MODEL // 2
    packed = (bits[:, half:] << 16) | bits[:, :half]
    return lax.bitcast_convert_type(packed, I32).reshape(N_TILES * SUBLANES, LANES)


def _pack_tile_rows(idx_t):
    rows = ((idx_t >> 1) * SUBLANES).T.astype(jnp.uint32)
    return lax.bitcast_convert_type((rows[:, 1::2] << 16) | rows[:, 0::2], I32)


def _slot_tile(rows_ref, tbl_ref, t, k):
    word = rows_ref[t, k // 2]
    row = lax.shift_right_logical(word, 16) if k % 2 else word & 0xFFFF
    return tbl_ref[pl.ds(pl.multiple_of(row, SUBLANES), SUBLANES), :]


def _unpack_tile(w):
    hi = pltpu.bitcast(w & I32(-65536), F32)
    lo = pltpu.bitcast(w << 16, F32)
    return hi, lo


def _peer_dots_kernel(tile_ref, x_ref, tbl_ref, out_ref, part_even, part_odd):
    sub = lax.broadcasted_iota(I32, (SUBLANES, LANES), 0)
    lane = lax.broadcasted_iota(I32, (SUBLANES, LANES), 1)
    low_half = sub < HALF_ROWS
    m2 = (sub & 2) == 0
    m1 = (sub & 1) == 0
    n_acc = 2 * (N_SLOTS // SLOT_GROUP)
    order = (0, 2, 1, 3, 4, 6, 5, 7)

    def partial_sums(t, part_ref):
        x = x_ref[t]
        xr = pltpu.roll(x, HALF_ROWS, 0)
        x_lo = jnp.where(low_half, x, xr)
        x_hi = jnp.where(low_half, xr, x)
        for g in range(N_SLOTS // SLOT_GROUP):
            ps = []
            for n in order:
                hi, lo = _unpack_tile(_slot_tile(tile_ref, tbl_ref, t, g * SLOT_GROUP + n))
                ps.append(hi * x_hi + lo * x_lo)
            st = [jnp.where(m2, ps[a] + pltpu.roll(ps[a], 6, 0), ps[a + 1] + pltpu.roll(ps[a + 1], 2, 0))
                  for a in range(0, SLOT_GROUP, 2)]
            for j, a in enumerate(range(0, SLOT_GROUP // 2, 2)):
                qv = jnp.where(m1, st[a] + pltpu.roll(st[a], 7, 0), st[a + 1] + pltpu.roll(st[a + 1], 1, 0))
                part_ref[(2 * g + j) * SUBLANES:(2 * g + j + 1) * SUBLANES, :] = qv

    def finish(t, part_ref, accs):
        this_tok = lane == t
        return tuple(
            jnp.where(this_tok, jnp.sum(part_ref[i * SUBLANES:(i + 1) * SUBLANES, :], axis=1, keepdims=True), acc)
            for i, acc in enumerate(accs))

    def token_pair(p, accs):
        t0 = 2 * p
        partial_sums(t0, part_even)
        accs = finish(t0 - 1, part_odd, accs)
        partial_sums(t0 + 1, part_odd)
        return finish(t0, part_even, accs)

    part_odd[...] = jnp.zeros_like(part_odd)
    zeros = tuple(jnp.zeros((SUBLANES, LANES), F32) for _ in range(n_acc))
    accs = lax.fori_loop(0, TB_PEER // 2, token_pair, zeros)
    accs = finish(TB_PEER - 1, part_odd, accs)
    for i, a in enumerate(accs):
        out_ref[i * SUBLANES:(i + 1) * SUBLANES, :] = a


def _peer_dots(tile_tok, x3, table):
    return pl.pallas_call(
        _peer_dots_kernel,
        grid=(N_TOK // TB_PEER,),
        in_specs=[
            pl.BlockSpec((TB_PEER, N_SLOTS // 2), lambda i: (i, 0), memory_space=pltpu.SMEM),
            pl.BlockSpec((TB_PEER, SUBLANES, LANES), lambda i: (i, 0, 0)),
            pl.BlockSpec((N_TILES * SUBLANES, LANES), lambda i: (0, 0), pipeline_mode=pl.Buffered(1)),
        ],
        out_specs=pl.BlockSpec((2 * N_SLOTS, TB_PEER), lambda i: (0, i)),
        out_shape=jax.ShapeDtypeStruct((2 * N_SLOTS, N_TOK), F32),
        scratch_shapes=[pltpu.VMEM((2 * N_SLOTS, LANES), F32), pltpu.VMEM((2 * N_SLOTS, LANES), F32)],
        compiler_params=_cparams(1),
        name="peer_dots",
    )(tile_tok, x3, table)


TM_ACT = 512


def _peer_act_kernel(da_ref, db_ref, idx_ref, gate_ref, wa_ref, wb_ref):
    odd = (idx_ref[...] & 1) == 1
    w = gate_ref[...] * _gelu_exact(jnp.where(odd, db_ref[...], da_ref[...]))
    zero = jnp.zeros_like(w)
    wa_ref[...] = jnp.where(odd, zero, w)
    wb_ref[...] = jnp.where(odd, w, zero)


def _peer_act(dots_a, dots_b, idx_t, gate_t):
    col = pl.BlockSpec((N_SLOTS, TM_ACT), lambda i: (0, i))
    return pl.pallas_call(
        _peer_act_kernel,
        grid=(N_TOK // TM_ACT,),
        in_specs=[col, col, col, col],
        out_specs=[col, col],
        out_shape=[jax.ShapeDtypeStruct((N_SLOTS, N_TOK), F32), jax.ShapeDtypeStruct((N_SLOTS, N_TOK), F32)],
        compiler_params=_cparams(1),
        name="peer_act",
    )(dots_a, dots_b, idx_t, gate_t)


N_VACC = 4


def _peer_combine_kernel(tile_ref, wa_ref, wb_ref, tbl_ref, out_ref, wa_even, wb_even, wa_odd, wb_odd):
    sub = lax.broadcasted_iota(I32, (SUBLANES, LANES), 0)
    lane = lax.broadcasted_iota(I32, (SUBLANES, LANES), 1)
    low_half = sub < HALF_ROWS

    def splat_weights(t, dst_a, dst_b):
        this_tok = lane == t
        for g in range(N_SLOTS // SUBLANES):
            rows = slice(g * SUBLANES, (g + 1) * SUBLANES)
            for src, dst in ((wa_ref, dst_a), (wb_ref, dst_b)):
                r = jnp.sum(jnp.where(this_tok, src[rows, :], 0.0), axis=1, keepdims=True)
                dst[rows, :] = jnp.broadcast_to(r, (SUBLANES, LANES))

    def combine(t, src_a, src_b):
        acc_hi = [jnp.zeros((SUBLANES, LANES), F32) for _ in range(N_VACC)]
        acc_lo = [jnp.zeros((SUBLANES, LANES), F32) for _ in range(N_VACC)]
        for k in range(N_SLOTS):
            hi, lo = _unpack_tile(_slot_tile(tile_ref, tbl_ref, t, k))
            wv = jnp.where(low_half, jnp.broadcast_to(src_a[k:k + 1, :], (SUBLANES, LANES)),
                           jnp.broadcast_to(src_b[k:k + 1, :], (SUBLANES, LANES)))
            a = k % N_VACC
            acc_hi[a] = acc_hi[a] + wv * hi
            acc_lo[a] = acc_lo[a] + wv * lo
        hi = (acc_hi[0] + acc_hi[1]) + (acc_hi[2] + acc_hi[3])
        lo = (acc_lo[0] + acc_lo[1]) + (acc_lo[2] + acc_lo[3])
        hi = hi + pltpu.roll(hi, HALF_ROWS, 0)
        lo = lo + pltpu.roll(lo, HALF_ROWS, 0)
        out_ref[t] = jnp.where(low_half, lo, hi)

    def token_pair(p, carry):
        t0 = 2 * p
        splat_weights(t0 + 1, wa_odd, wb_odd)
        combine(t0, wa_even, wb_even)
        splat_weights(jnp.minimum(t0 + 2, TB_PEER - 1), wa_even, wb_even)
        combine(t0 + 1, wa_odd, wb_odd)
        return carry

    splat_weights(0, wa_even, wb_even)
    lax.fori_loop(0, TB_PEER // 2, token_pair, 0)


def _peer_combine(tile_tok, wa_t, wb_t, table):
    col = pl.BlockSpec((N_SLOTS, TB_PEER), lambda i: (0, i))
    return pl.pallas_call(
        _peer_combine_kernel,
        grid=(N_TOK // TB_PEER,),
        in_specs=[
            pl.BlockSpec((TB_PEER, N_SLOTS // 2), lambda i: (i, 0), memory_space=pltpu.SMEM),
            col, col,
            pl.BlockSpec((N_TILES * SUBLANES, LANES), lambda i: (0, 0), pipeline_mode=pl.Buffered(1)),
        ],
        out_specs=pl.BlockSpec((TB_PEER, SUBLANES, LANES), lambda i: (i, 0, 0)),
        out_shape=jax.ShapeDtypeStruct((N_TOK, SUBLANES, LANES), F32),
        scratch_shapes=[pltpu.VMEM((N_SLOTS, LANES), F32) for _ in range(4)],
        compiler_params=_cparams(1),
        name="peer_combine",
    )(tile_tok, wa_t, wb_t, table)


def _peer(x2, wq_bf, keys_bf, u_packed, v_packed):
    idx_t, gate_t = _peer_route(x2, wq_bf, keys_bf)
    tile_tok = _pack_tile_rows(idx_t)
    dots = _peer_dots(tile_tok, x2.reshape(N_TOK, SUBLANES, LANES), u_packed)
    d4 = dots.reshape(N_SLOTS // HALF_ROWS, 2, HALF_ROWS, N_TOK)
    wa_t, wb_t = _peer_act(d4[:, 0].reshape(N_SLOTS, N_TOK), d4[:, 1].reshape(N_SLOTS, N_TOK), idx_t, gate_t)
    out3 = _peer_combine(tile_tok, wa_t, wb_t, v_packed)
    return out3.reshape(N_TOK, D_MODEL)


TM_SGU = 256


def _sgu_kernel(h1_ref, peer_ref, mod0_ref, mod_ref, g1_ref, g2_ref, win_ref, lng_ref, lnb_ref, ws_ref, bs_ref,
                wout_ref, h3_ref, x2_ref):
    h2 = h1_ref[...] + mod0_ref[0, 5:6, :] * peer_ref[...]
    hm = _rms_modulate(h2, g1_ref[...], mod_ref[0, 0:1, :], mod_ref[0, 1:2, :])
    z = _gelu_exact(jnp.dot(hm.astype(BF16), win_ref[...], preferred_element_type=F32))
    u = z[:, :D_SGU]
    vv = z[:, D_SGU:]
    mu = jnp.mean(vv, axis=-1, keepdims=True)
    d = vv - mu
    var = jnp.mean(d * d, axis=-1, keepdims=True)
    vn = (d * lax.rsqrt(var + EPS) * lng_ref[...] + lnb_ref[...]).astype(BF16)
    gw = D_SGU // SGU_GROUPS
    rows = []
    for c in range(TM_SGU // CHUNK):
        cols = []
        for g in range(SGU_GROUPS):
            blk = vn[c * CHUNK:(c + 1) * CHUNK, g * gw:(g + 1) * gw]
            cols.append(jnp.dot(ws_ref[g], blk, preferred_element_type=F32) + bs_ref[g])
        rows.append(jnp.concatenate(cols, axis=1))
    mixed = jnp.concatenate(rows, axis=0)
    mix = jnp.dot((u * mixed).astype(BF16), wout_ref[...], preferred_element_type=F32)
    h3 = h2 + mod_ref[0, 2:3, :] * mix
    h3_ref[...] = h3
    x2_ref[...] = _rms_modulate(h3, g2_ref[...], mod_ref[0, 3:4, :], mod_ref[0, 4:5, :])


def _sgu_layer(h1, peer0, mod0, mod1, g1, g2, win_bf, lng, lnb, ws_bf, bs, wout_bf):
    tiles_per_b = SEQ // TM_SGU
    tok = lambda i: (i, 0)
    full = lambda i: (0, 0)
    modspec = pl.BlockSpec((1, N_MOD, D_MODEL), lambda i: (i // tiles_per_b, 0, 0))
    vec = pl.BlockSpec((1, D_MODEL), full)
    return pl.pallas_call(
        _sgu_kernel,
        grid=(N_TOK // TM_SGU,),
        in_specs=[
            pl.BlockSpec((TM_SGU, D_MODEL), tok),
            pl.BlockSpec((TM_SGU, D_MODEL), tok),
            modspec, modspec, vec, vec,
            pl.BlockSpec((D_MODEL, 2 * D_SGU), full),
            vec, vec,
            pl.BlockSpec((SGU_GROUPS, CHUNK, CHUNK), lambda i: (0, 0, 0)),
            pl.BlockSpec((SGU_GROUPS, CHUNK, 1), lambda i: (0, 0, 0)),
            pl.BlockSpec((D_SGU, D_MODEL), full),
        ],
        out_specs=[pl.BlockSpec((TM_SGU, D_MODEL), tok), pl.BlockSpec((TM_SGU, D_MODEL), tok)],
        out_shape=[jax.ShapeDtypeStruct((N_TOK, D_MODEL), F32), jax.ShapeDtypeStruct((N_TOK, D_MODEL), F32)],
        compiler_params=_cparams(1),
        name="sgu_layer",
    )(h1, peer0, mod0, mod1, g1, g2, win_bf, lng, lnb, ws_bf, bs, wout_bf)


TM_FINAL = 512


def _final_kernel(h_ref, peer_ref, mod_ref, g_ref, o_ref):
    h = h_ref[...] + mod_ref[0, 5:6, :] * peer_ref[...]
    o_ref[...] = (h * lax.rsqrt(jnp.mean(h * h, axis=-1, keepdims=True) + EPS)) * g_ref[...]


def _final(h3, peer1, mod1, gf):
    tiles_per_b = SEQ // TM_FINAL
    tok = lambda i: (i, 0)
    return pl.pallas_call(
        _final_kernel,
        grid=(N_TOK // TM_FINAL,),
        in_specs=[
            pl.BlockSpec((TM_FINAL, D_MODEL), tok),
            pl.BlockSpec((TM_FINAL, D_MODEL), tok),
            pl.BlockSpec((1, N_MOD, D_MODEL), lambda i: (i // tiles_per_b, 0, 0)),
            pl.BlockSpec((1, D_MODEL), lambda i: (0, 0)),
        ],
        out_specs=pl.BlockSpec((TM_FINAL, D_MODEL), tok),
        out_shape=jax.ShapeDtypeStruct((N_TOK, D_MODEL), F32),
        compiler_params=_cparams(1),
        name="final_norm",
    )(h3, peer1, mod1, gf)


def kernel(x, c, ctx, c_ctx, norm1_g, norm2_g, ada_w, ada_b, ab_w_in, conv_w, conv_b, conv_gn_g, conv_gn_b,
           na_rpb, ab_w_out, sgu_w_in, sgu_ln_g, sgu_ln_b, sgu_w_s, sgu_b_s, sgu_w_out, peer_wq, peer_keys,
           peer_u, peer_v, norm_f_g):
    assert x.shape == (BATCH, SEQ, D_MODEL) and ctx.shape == (BATCH, CTX_LEN, D_MODEL)
    x2d = x.reshape(N_TOK, D_MODEL)
    row = lambda a: a.reshape(1, -1)

    cc = jnp.concatenate([c, c_ctx[None], jnp.zeros((16 - BATCH - 1, D_MODEL), F32)], axis=0)
    mod_all = _ada_modulation(cc, ada_w, ada_b)
    mod0 = mod_all[0, :BATCH].reshape(BATCH, N_MOD, D_MODEL)
    mod0_c = mod_all[0, BATCH].reshape(N_MOD, D_MODEL)
    mod1 = mod_all[1, :BATCH].reshape(BATCH, N_MOD, D_MODEL)

    w_in = ab_w_in[0].astype(BF16)
    hglu, q, k, v = _inproj0(x2d, mod0, row(norm1_g[0]), w_in)
    kc, vc = _ctx_kv(ctx.reshape(BATCH * CTX_LEN, D_MODEL), mod0_c, row(norm1_g[0]), w_in[:, 2 * D_CONV + D_ATTN:])
    y_a = _conv_module(hglu.reshape(BATCH, SEQ, D_CONV), conv_w[0], conv_b[0], conv_gn_g[0], conv_gn_b[0])
    y_b = _neighbourhood_attention(
        q, k.reshape(BATCH, SEQ, D_ATTN), v.reshape(BATCH, SEQ, D_ATTN),
        kc.reshape(BATCH, CTX_LEN, D_ATTN), vc.reshape(BATCH, CTX_LEN, D_ATTN), _na_bias_table(na_rpb[0]))
    w_out = ab_w_out[0].astype(BF16)
    h1, x2 = _outproj0(y_a.reshape(N_TOK, D_CONV), y_b, x2d, mod0, row(norm2_g[0]), w_out[:D_CONV], w_out[D_CONV:])
    peer0 = _peer(x2, peer_wq[0].astype(BF16), peer_keys[0].astype(BF16),
                  _pack_expert_table(peer_u[0]), _pack_expert_table(peer_v[0]))

    h3, x2b = _sgu_layer(
        h1, peer0, mod0, mod1, row(norm1_g[1]), row(norm2_g[1]), sgu_w_in[0].astype(BF16), row(sgu_ln_g[0]),
        row(sgu_ln_b[0]), sgu_w_s[0].astype(BF16), sgu_b_s[0].reshape(SGU_GROUPS, CHUNK, 1),
        sgu_w_out[0].astype(BF16))
    peer1 = _peer(x2b, peer_wq[1].astype(BF16), peer_keys[1].astype(BF16),
                  _pack_expert_table(peer_u[1]), _pack_expert_table(peer_v[1]))

    out = _final(h3, peer1, mod1, row(norm_f_g))
    return out.reshape(BATCH, SEQ, D_MODEL)
```

```python
import functools

import jax
import jax.numpy as jnp
from jax import lax
from jax.experimental import pallas as pl
from jax.experimental.pallas import tpu as pltpu

F32 = jnp.float32
BF16 = jnp.bfloat16
I32 = jnp.int32

D_MODEL = 1024
BATCH = 8
SEQ = 4096
DEPTH = 2
GRID_W = 64
GRID_H = SEQ // GRID_W
CTX_LEN = 256
N_MOD = 6
EPS = 1e-6
D_CONV = 512
CONV_WIDTH = 31
CONV_HALF = CONV_WIDTH // 2
CONV_GROUPS = 8
NA_HEADS = 8
HEAD_DIM = 64
D_ATTN = NA_HEADS * HEAD_DIM
NA_ROWS = 8
NA_COLS = 16
D_AB_IN = 2 * D_CONV + 3 * D_ATTN
CHUNK = 128
SGU_GROUPS = 8
D_SGU = 1024
PEER_HEADS = 8
N_KEYS = 128
N_EXPERTS = N_KEYS * N_KEYS
PEER_TOPK = 16
D_KEY = 128
N_SLOTS = PEER_HEADS * PEER_TOPK

N_TOK = BATCH * SEQ

SUBLANES = 8
LANES = 128
VMEM_LIMIT = 56 * 1024 * 1024

NEG_BIG = -1e30


def _cparams(n_axes, vmem=VMEM_LIMIT):
    return pltpu.CompilerParams(dimension_semantics=("parallel",) * n_axes, vmem_limit_bytes=vmem)


def _rms_modulate(xf, g, shift, scale):
    y = xf * lax.rsqrt(jnp.mean(xf * xf, axis=-1, keepdims=True) + EPS)
    return (y * g) * (1.0 + scale) + shift


def _gelu_exact(x):
    return 0.5 * x * (1.0 + lax.erf(x * (2.0 ** -0.5)))


def _silu(x):
    return x * jax.nn.sigmoid(x)


ADA_TN = 1536


def _ada_kernel(c_ref, w_ref, b_ref, o_ref):
    s = _silu(c_ref[...])
    o_ref[0] = jnp.dot(s.astype(BF16), w_ref[0].astype(BF16), preferred_element_type=F32) + b_ref[0]


def _ada_modulation(cc, ada_w, ada_b):
    n = N_MOD * D_MODEL
    return pl.pallas_call(
        _ada_kernel,
        grid=(DEPTH, n // ADA_TN),
        in_specs=[
            pl.BlockSpec((16, D_MODEL), lambda l, j: (0, 0)),
            pl.BlockSpec((1, D_MODEL, ADA_TN), lambda l, j: (l, 0, j)),
            pl.BlockSpec((1, 1, ADA_TN), lambda l, j: (l, 0, j)),
        ],
        out_specs=pl.BlockSpec((1, 16, ADA_TN), lambda l, j: (l, 0, j)),
        out_shape=jax.ShapeDtypeStruct((DEPTH, 16, n), F32),
        compiler_params=_cparams(2),
        name="ada_modulation",
    )(cc, ada_w, ada_b.reshape(DEPTH, 1, n))


TM_PROJ = 512


def _inproj0_kernel(x_ref, mod_ref, g_ref, w_ref, hglu_ref, q_ref, k_ref, v_ref):
    hm = _rms_modulate(x_ref[...], g_ref[...], mod_ref[0, 0:1, :], mod_ref[0, 1:2, :])
    proj = jnp.dot(hm.astype(BF16), w_ref[...], preferred_element_type=F32)
    a = proj[:, :D_CONV]
    gate = proj[:, D_CONV:2 * D_CONV]
    hglu_ref[...] = a * jax.nn.sigmoid(gate)
    o = 2 * D_CONV
    q_ref[...] = proj[:, o:o + D_ATTN].astype(BF16)
    k_ref[...] = proj[:, o + D_ATTN:o + 2 * D_ATTN].astype(BF16)
    v_ref[...] = proj[:, o + 2 * D_ATTN:o + 3 * D_ATTN].astype(BF16)


def _inproj0(x2d, mod, g, w_in_bf):
    tiles_per_b = SEQ // TM_PROJ
    tok = lambda i: (i, 0)
    return pl.pallas_call(
        _inproj0_kernel,
        grid=(N_TOK // TM_PROJ,),
        in_specs=[
            pl.BlockSpec((TM_PROJ, D_MODEL), tok),
            pl.BlockSpec((1, N_MOD, D_MODEL), lambda i: (i // tiles_per_b, 0, 0)),
            pl.BlockSpec((1, D_MODEL), lambda i: (0, 0)),
            pl.BlockSpec((D_MODEL, D_AB_IN), lambda i: (0, 0)),
        ],
        out_specs=[
            pl.BlockSpec((TM_PROJ, D_CONV), tok),
            pl.BlockSpec((TM_PROJ, D_ATTN), tok),
            pl.BlockSpec((TM_PROJ, D_ATTN), tok),
            pl.BlockSpec((TM_PROJ, D_ATTN), tok),
        ],
        out_shape=[
            jax.ShapeDtypeStruct((N_TOK, D_CONV), F32),
            jax.ShapeDtypeStruct((N_TOK, D_ATTN), BF16),
            jax.ShapeDtypeStruct((N_TOK, D_ATTN), BF16),
            jax.ShapeDtypeStruct((N_TOK, D_ATTN), BF16),
        ],
        compiler_params=_cparams(1),
        name="inproj0",
    )(x2d, mod, g, w_in_bf)


def _ctx_kv_kernel(x_ref, mod_ref, g_ref, w_ref, k_ref, v_ref):
    hm = _rms_modulate(x_ref[...], g_ref[...], mod_ref[0:1, :], mod_ref[1:2, :])
    proj = jnp.dot(hm.astype(BF16), w_ref[...], preferred_element_type=F32)
    k_ref[...] = proj[:, :D_ATTN].astype(BF16)
    v_ref[...] = proj[:, D_ATTN:].astype(BF16)


def _ctx_kv(ctx2d, mod_c, g, w_kv_bf):
    n = ctx2d.shape[0]
    tm = 512
    tok = lambda i: (i, 0)
    return pl.pallas_call(
        _ctx_kv_kernel,
        grid=(n // tm,),
        in_specs=[
            pl.BlockSpec((tm, D_MODEL), tok),
            pl.BlockSpec((N_MOD, D_MODEL), lambda i: (0, 0)),
            pl.BlockSpec((1, D_MODEL), lambda i: (0, 0)),
            pl.BlockSpec((D_MODEL, 2 * D_ATTN), lambda i: (0, 0)),
        ],
        out_specs=[pl.BlockSpec((tm, D_ATTN), tok), pl.BlockSpec((tm, D_ATTN), tok)],
        out_shape=[jax.ShapeDtypeStruct((n, D_ATTN), BF16), jax.ShapeDtypeStruct((n, D_ATTN), BF16)],
        compiler_params=_cparams(1),
        name="ctx_kv",
    )(ctx2d, mod_c, g, w_kv_bf)


TL_CONV = 256
CONV_SUB = 64
CONV_PAD = 16


def _split_dot(x, a_bf):
    hi = x.astype(BF16)
    lo = (x - hi.astype(F32)).astype(BF16)
    return (jnp.dot(hi, a_bf, preferred_element_type=F32) + jnp.dot(lo, a_bf, preferred_element_type=F32))


def _conv_kernel(prev_ref, cur_ref, next_ref, cw_ref, cb_ref, gg_ref, gb_ref, avg_ref, o_ref, win_ref):
    j = pl.program_id(1)
    nj = pl.num_programs(1)
    zeros = jnp.zeros((CONV_PAD, D_CONV), F32)
    win_ref[0:CONV_PAD, :] = jnp.where(j > 0, prev_ref[0, TL_CONV - CONV_PAD:TL_CONV, :], zeros)
    win_ref[CONV_PAD:CONV_PAD + TL_CONV, :] = cur_ref[0]
    win_ref[CONV_PAD + TL_CONV:2 * CONV_PAD + TL_CONV, :] = jnp.where(j < nj - 1, next_ref[0, 0:CONV_PAD, :], zeros)
    avg = avg_ref[...]
    for r0 in range(0, TL_CONV, CONV_SUB):
        acc = jnp.zeros((CONV_SUB, D_CONV), F32) + cb_ref[...]
        for k in range(CONV_WIDTH):
            off = r0 + CONV_PAD - CONV_HALF + k
            acc = acc + win_ref[off:off + CONV_SUB, :] * cw_ref[k:k + 1, :]
        mu = _split_dot(acc, avg)
        d = acc - mu
        var = _split_dot(d * d, avg)
        y = d * lax.rsqrt(var + EPS) * gg_ref[...] + gb_ref[...]
        o_ref[0, r0:r0 + CONV_SUB, :] = _silu(y).astype(BF16)


def _conv_module(hglu3, cw, cb, gg, gb):
    nj = SEQ // TL_CONV
    gsz = D_CONV // CONV_GROUPS
    gid = jnp.arange(D_CONV) // gsz
    avg = jnp.where(gid[:, None] == gid[None, :], 1.0 / gsz, 0.0).astype(BF16)
    vec = pl.BlockSpec((1, D_CONV), lambda b, j: (0, 0))
    return pl.pallas_call(
        _conv_kernel,
        grid=(BATCH, nj),
        in_specs=[
            pl.BlockSpec((1, TL_CONV, D_CONV), lambda b, j: (b, jnp.maximum(j - 1, 0), 0)),
            pl.BlockSpec((1, TL_CONV, D_CONV), lambda b, j: (b, j, 0)),
            pl.BlockSpec((1, TL_CONV, D_CONV), lambda b, j: (b, jnp.minimum(j + 1, nj - 1), 0)),
            pl.BlockSpec((CONV_WIDTH, D_CONV), lambda b, j: (0, 0)),
            vec, vec, vec,
            pl.BlockSpec((D_CONV, D_CONV), lambda b, j: (0, 0)),
        ],
        out_specs=pl.BlockSpec((1, TL_CONV, D_CONV), lambda b, j: (b, j, 0)),
        out_shape=jax.ShapeDtypeStruct((BATCH, SEQ, D_CONV), BF16),
        scratch_shapes=[pltpu.VMEM((TL_CONV + 2 * CONV_PAD, D_CONV), F32)],
        compiler_params=_cparams(2),
        name="conv_module",
    )(hglu3, hglu3, hglu3, cw, cb.reshape(1, D_CONV), gg.reshape(1, D_CONV), gb.reshape(1, D_CONV), avg)


NA_WIN = NA_ROWS * GRID_W


def _na_kernel(q_ref, k_ref, v_ref, kc_ref, vc_ref, bias_ref, o_ref):
    r = pl.program_id(1)
    kr0 = jnp.clip(r - NA_ROWS // 2, 0, GRID_H - NA_ROWS)
    start = pl.multiple_of(kr0 * GRID_W, GRID_W)
    scale = HEAD_DIM ** -0.5
    lane = lax.broadcasted_iota(I32, (GRID_W, LANES), 1)
    for hp in range(NA_HEADS // 2):
        ls = slice(hp * LANES, (hp + 1) * LANES)
        qp = q_ref[:, ls]
        kw = k_ref[0, pl.ds(start, NA_WIN), ls]
        vw = v_ref[0, pl.ds(start, NA_WIN), ls]
        kc = kc_ref[0, :, ls]
        vc = vc_ref[0, :, ls]
        outs = []
        for hh in range(2):
            in_head = (lane >= hh * HEAD_DIM) & (lane < (hh + 1) * HEAD_DIM)
            qm = jnp.where(in_head, qp, jnp.zeros_like(qp))
            dn = (((1,), (1,)), ((), ()))
            s_win = lax.dot_general(qm, kw, dn, preferred_element_type=F32) * scale + bias_ref[0, 2 * hp + hh]
            s_ctx = lax.dot_general(qm, kc, dn, preferred_element_type=F32) * scale
            m = jnp.maximum(jnp.max(s_win, axis=-1, keepdims=True), jnp.max(s_ctx, axis=-1, keepdims=True))
            p_win = jnp.exp(s_win - m)
            p_ctx = jnp.exp(s_ctx - m)
            inv = 1.0 / (jnp.sum(p_win, axis=-1, keepdims=True) + jnp.sum(p_ctx, axis=-1, keepdims=True))
            o = (jnp.dot((p_win * inv).astype(BF16), vw, preferred_element_type=F32)
                 + jnp.dot((p_ctx * inv).astype(BF16), vc, preferred_element_type=F32))
            outs.append(o)
        o_ref[:, ls] = jnp.where(lane < HEAD_DIM, outs[0], outs[1]).astype(BF16)


def _na_bias_table(rpb):
    qc = jnp.arange(GRID_W)
    kc = jnp.arange(GRID_W)
    col_start = jnp.clip(qc - NA_COLS // 2, 0, GRID_W - NA_COLS)
    valid = (kc[None, :] >= col_start[:, None]) & (kc[None, :] < col_start[:, None] + NA_COLS)
    dc = jnp.clip(kc[None, :] - qc[:, None] + (NA_COLS - 1), 0, 2 * NA_COLS - 2)
    pick = (dc[:, :, None] == jnp.arange(2 * NA_COLS - 1)).astype(F32)
    by_col = jnp.einsum("hab,qkb->haqk", rpb.astype(F32), pick, precision=lax.Precision.HIGHEST)
    by_col = jnp.where(valid[None, None], by_col, F32(NEG_BIG))
    tab = jnp.stack([by_col[:, v:v + NA_ROWS] for v in range(NA_ROWS)], axis=0)
    return jnp.transpose(tab, (0, 1, 3, 2, 4)).reshape(NA_ROWS, NA_HEADS, GRID_W, NA_WIN)


def _neighbourhood_attention(q, k3, v3, kc3, vc3, bias_tab):
    def bias_map(b, r):
        kr0 = jnp.clip(r - NA_ROWS // 2, 0, GRID_H - NA_ROWS)
        return (kr0 - r + (NA_ROWS - 1), 0, 0, 0)

    return pl.pallas_call(
        _na_kernel,
        grid=(BATCH, GRID_H),
        in_specs=[
            pl.BlockSpec((GRID_W, D_ATTN), lambda b, r: (b * GRID_H + r, 0)),
            pl.BlockSpec((1, SEQ, D_ATTN), lambda b, r: (b, 0, 0)),
            pl.BlockSpec((1, SEQ, D_ATTN), lambda b, r: (b, 0, 0)),
            pl.BlockSpec((1, CTX_LEN, D_ATTN), lambda b, r: (b, 0, 0)),
            pl.BlockSpec((1, CTX_LEN, D_ATTN), lambda b, r: (b, 0, 0)),
            pl.BlockSpec((1, NA_HEADS, GRID_W, NA_WIN), bias_map),
        ],
        out_specs=pl.BlockSpec((GRID_W, D_ATTN), lambda b, r: (b * GRID_H + r, 0)),
        out_shape=jax.ShapeDtypeStruct((N_TOK, D_ATTN), BF16),
        compiler_params=_cparams(2),
        name="natten",
    )(q, k3, v3, kc3, vc3, bias_tab)


TM_OUT = 512


def _outproj0_kernel(ya_ref, yb_ref, h_ref, mod_ref, g2_ref, wa_ref, wb_ref, h1_ref, x2_ref):
    mix = (jnp.dot(ya_ref[...], wa_ref[...], preferred_element_type=F32)
           + jnp.dot(yb_ref[...], wb_ref[...], preferred_element_type=F32))
    h1 = h_ref[...] + mod_ref[0, 2:3, :] * mix
    h1_ref[...] = h1
    x2_ref[...] = _rms_modulate(h1, g2_ref[...], mod_ref[0, 3:4, :], mod_ref[0, 4:5, :])


def _outproj0(ya, yb, h, mod, g2, wa_bf, wb_bf):
    tiles_per_b = SEQ // TM_OUT
    tok = lambda i: (i, 0)
    full = lambda i: (0, 0)
    return pl.pallas_call(
        _outproj0_kernel,
        grid=(N_TOK // TM_OUT,),
        in_specs=[
            pl.BlockSpec((TM_OUT, D_CONV), tok),
            pl.BlockSpec((TM_OUT, D_ATTN), tok),
            pl.BlockSpec((TM_OUT, D_MODEL), tok),
            pl.BlockSpec((1, N_MOD, D_MODEL), lambda i: (i // tiles_per_b, 0, 0)),
            pl.BlockSpec((1, D_MODEL), full),
            pl.BlockSpec((D_CONV, D_MODEL), full),
            pl.BlockSpec((D_ATTN, D_MODEL), full),
        ],
        out_specs=[pl.BlockSpec((TM_OUT, D_MODEL), tok), pl.BlockSpec((TM_OUT, D_MODEL), tok)],
        out_shape=[jax.ShapeDtypeStruct((N_TOK, D_MODEL), F32), jax.ShapeDtypeStruct((N_TOK, D_MODEL), F32)],
        compiler_params=_cparams(1),
        name="outproj0",
    )(ya, yb, h, mod, g2, wa_bf, wb_bf)


TM_ROUTE = 256


def _topk_cols(s, vals_of, k):
    n, tm = s.shape
    rows = lax.broadcasted_iota(I32, (n, tm), 0).astype(F32)
    krow = lax.broadcasted_iota(I32, (k, tm), 0)
    top = jnp.zeros((k, tm), F32)
    pay = None
    for it in range(k):
        m = jnp.max(s, axis=0, keepdims=True)
        pos = jnp.min(jnp.where(s == m, rows, F32(n)), axis=0, keepdims=True)
        sel = rows == pos
        extra = vals_of(sel, pos)
        if pay is None:
            pay = [jnp.zeros((k, tm), F32) for _ in extra]
        top = jnp.where(krow == it, m, top)
        pay = [jnp.where(krow == it, e, p) for e, p in zip(extra, pay)]
        s = jnp.where(sel, -jnp.inf, s)
    return top, pay


def _route_kernel(x_ref, wq_ref, keys_ref, idx_ref, gate_ref, rows_ref):
    q = jnp.dot(x_ref[...].astype(BF16), wq_ref[...], preferred_element_type=F32).astype(BF16)
    dn = (((1,), (1,)), ((), ()))
    for h in range(PEER_HEADS):
        halves = []
        for p in range(2):
            c0 = (2 * h + p) * D_KEY
            st = lax.dot_general(keys_ref[h, p], q[:, c0:c0 + D_KEY], dn, preferred_element_type=F32)
            top, (ki,) = _topk_cols(st, lambda sel, pos: [pos], PEER_TOPK)
            halves.append((top, ki))
        (s1, i1), (s2, i2) = halves
        neg = jnp.full((SUBLANES, s1.shape[1]), -jnp.inf, F32)
        brow = lax.broadcasted_iota(I32, neg.shape, 0)
        cand_blocks, cidx_blocks = [], []
        for a in range(SUBLANES):
            nb = PEER_TOPK // (a + 1)
            rows_b = PEER_TOPK if nb > SUBLANES else SUBLANES
            sa = s1[a:a + 1, :] + s2[:rows_b]
            if nb < rows_b:
                sa = jnp.where(brow < nb, sa, neg)
            cand_blocks.append(sa)
            cidx_blocks.append(i1[a:a + 1, :] * F32(N_KEYS) + i2[:rows_b])
        cand_blocks.append(s1[SUBLANES:] + s2[0:1, :])
        cidx_blocks.append(i1[SUBLANES:] * F32(N_KEYS) + i2[0:1, :])
        cand = jnp.concatenate(cand_blocks, axis=0)
        cidx = jnp.concatenate(cidx_blocks, axis=0)
        top_s, (eidx,) = _topk_cols(
            cand, lambda sel, pos: [jnp.max(jnp.where(sel, cidx, -1.0), axis=0, keepdims=True)], PEER_TOPK)
        e = jnp.exp(top_s - top_s[0:1, :])
        gate = e / jnp.sum(e, axis=0, keepdims=True)
        idx_ref[h * PEER_TOPK:(h + 1) * PEER_TOPK, :] = eidx.astype(I32)
        gate_ref[h * PEER_TOPK:(h + 1) * PEER_TOPK, :] = gate
    rows = (idx_ref[...] >> 1) * SUBLANES
    rows_ref[...] = rows[:N_SLOTS // 2] | (rows[N_SLOTS // 2:] << 16)


def _peer_route(x2, wq_bf, keys_bf):
    col = lambda i: (0, i)
    return pl.pallas_call(
        _route_kernel,
        grid=(N_TOK // TM_ROUTE,),
        in_specs=[
            pl.BlockSpec((TM_ROUTE, D_MODEL), lambda i: (i, 0)),
            pl.BlockSpec((D_MODEL, PEER_HEADS * 2 * D_KEY), lambda i: (0, 0)),
            pl.BlockSpec((PEER_HEADS, 2, N_KEYS, D_KEY), lambda i: (0, 0, 0, 0)),
        ],
        out_specs=[pl.BlockSpec((N_SLOTS, TM_ROUTE), col), pl.BlockSpec((N_SLOTS, TM_ROUTE), col),
                   pl.BlockSpec((N_SLOTS // 2, TM_ROUTE), col)],
        out_shape=[jax.ShapeDtypeStruct((N_SLOTS, N_TOK), I32), jax.ShapeDtypeStruct((N_SLOTS, N_TOK), F32),
                   jax.ShapeDtypeStruct((N_SLOTS // 2, N_TOK), I32)],
        compiler_params=_cparams(1),
        name="peer_route",
    )(x2, wq_bf, keys_bf)


N_TILES = N_EXPERTS // 2
HALF_ROWS = SUBLANES // 2
TB_PEER = 128
SLOT_GROUP = 8


def _pack_expert_table(tab):
    bits = lax.bitcast_convert_type(tab.astype(BF16), jnp.uint16).astype(jnp.uint32)
    half = D_MODEL // 2
    packed = (bits[:, half:] << 16) | bits[:, :half]
    return lax.bitcast_convert_type(packed, I32).reshape(N_TILES * SUBLANES, LANES)


def _slot_rows(rows_ref, t):
    for j in range(N_SLOTS // 2):
        word = rows_ref[t, j]
        yield j, word & 0xFFFF
        yield j + N_SLOTS // 2, lax.shift_right_logical(word, 16)


def _load_tile(tbl_ref, row):
    return tbl_ref[pl.ds(pl.multiple_of(row, SUBLANES), SUBLANES), :]


def _unpack_tile(w):
    hi = pltpu.bitcast(w & I32(-65536), F32)
    lo = pltpu.bitcast(w << 16, F32)
    return hi, lo


def _peer_dots_kernel(tile_ref, x_ref, tbl_ref, da_ref, db_ref, prod_even, prod_odd):
    sub = lax.broadcasted_iota(I32, (SUBLANES, LANES), 0)
    lane = lax.broadcasted_iota(I32, (SUBLANES, LANES), 1)
    low_half = sub < HALF_ROWS

    def products(t, prod_ref):
        x = x_ref[t]
        xr = pltpu.roll(x, HALF_ROWS, 0)
        x_lo = jnp.where(low_half, x, xr)
        x_hi = jnp.where(low_half, xr, x)
        for k, row in _slot_rows(tile_ref, t):
            hi, lo = _unpack_tile(_load_tile(tbl_ref, row))
            prod_ref[k * SUBLANES:(k + 1) * SUBLANES, :] = hi * x_hi + lo * x_lo

    def reduce_store(t, prod_ref):
        acc = jnp.zeros((SUBLANES, LANES), F32)
        for k in range(N_SLOTS):
            r = jnp.sum(prod_ref[k * SUBLANES:(k + 1) * SUBLANES, :], axis=1, keepdims=True)
            acc = jnp.where(lane == k, r, acc)
        acc = acc + pltpu.roll(acc, 6, 0)
        acc = acc + pltpu.roll(acc, 7, 0)
        da_ref[pl.ds(t, 1), :] = acc[0:1, :]
        db_ref[pl.ds(t, 1), :] = acc[HALF_ROWS:HALF_ROWS + 1, :]

    def token_pair(p, carry):
        t0 = 2 * p
        products(t0 + 1, prod_odd)
        reduce_store(t0, prod_even)
        products(jnp.minimum(t0 + 2, TB_PEER - 1), prod_even)
        reduce_store(t0 + 1, prod_odd)
        return carry

    products(0, prod_even)
    lax.fori_loop(0, TB_PEER // 2, token_pair, 0)


def _peer_dots(tile_tok, x3, table):
    out = pl.BlockSpec((TB_PEER, N_SLOTS), lambda i: (i, 0))
    return pl.pallas_call(
        _peer_dots_kernel,
        grid=(N_TOK // TB_PEER,),
        in_specs=[
            pl.BlockSpec((TB_PEER, N_SLOTS // 2), lambda i: (i, 0), memory_space=pltpu.SMEM),
            pl.BlockSpec((TB_PEER, SUBLANES, LANES), lambda i: (i, 0, 0)),
            pl.BlockSpec((N_TILES * SUBLANES, LANES), lambda i: (0, 0), pipeline_mode=pl.Buffered(1)),
        ],
        out_specs=[out, out],
        out_shape=[jax.ShapeDtypeStruct((N_TOK, N_SLOTS), F32), jax.ShapeDtypeStruct((N_TOK, N_SLOTS), F32)],
        scratch_shapes=[pltpu.VMEM((N_SLOTS * SUBLANES, LANES), F32), pltpu.VMEM((N_SLOTS * SUBLANES, LANES), F32)],
        compiler_params=_cparams(1),
        name="peer_dots",
    )(tile_tok, x3, table)


TM_ACT = 512


def _peer_act_kernel(da_ref, db_ref, idx_ref, gate_ref, wa_ref, wb_ref):
    odd = (idx_ref[...] & 1) == 1
    w = gate_ref[...] * _gelu_exact(jnp.where(odd, db_ref[...].T, da_ref[...].T))
    bits = pltpu.bitcast(w.astype(BF16).astype(F32), I32)
    w = pltpu.bitcast(bits | lax.shift_right_logical(bits, 16), F32)
    zero = jnp.zeros_like(w)
    wa_ref[...] = jnp.where(odd, zero, w)
    wb_ref[...] = jnp.where(odd, w, zero)


def _peer_act(dots_a, dots_b, idx_t, gate_t):
    col = pl.BlockSpec((N_SLOTS, TM_ACT), lambda i: (0, i))
    row = pl.BlockSpec((TM_ACT, N_SLOTS), lambda i: (i, 0))
    return pl.pallas_call(
        _peer_act_kernel,
        grid=(N_TOK // TM_ACT,),
        in_specs=[row, row, col, col],
        out_specs=[col, col],
        out_shape=[jax.ShapeDtypeStruct((N_SLOTS, N_TOK), F32), jax.ShapeDtypeStruct((N_SLOTS, N_TOK), F32)],
        compiler_params=_cparams(1),
        name="peer_act",
    )(dots_a, dots_b, idx_t, gate_t)


N_VACC = 4
BF16_GROUP = 4


def _peer_combine_kernel(tile_ref, wa_ref, wb_ref, tbl_ref, out_ref, wa_even, wb_even, wa_odd, wb_odd):
    sub = lax.broadcasted_iota(I32, (SUBLANES, LANES), 0)
    lane = lax.broadcasted_iota(I32, (SUBLANES, LANES), 1)
    low_half = sub < HALF_ROWS

    def splat_weights(t, dst_a, dst_b):
        this_tok = lane == t
        for g in range(N_SLOTS // SUBLANES):
            rows = slice(g * SUBLANES, (g + 1) * SUBLANES)
            for src, dst in ((wa_ref, dst_a), (wb_ref, dst_b)):
                r = jnp.sum(jnp.where(this_tok, src[rows, :], 0.0), axis=1, keepdims=True)
                dst[rows, :] = jnp.broadcast_to(r, (SUBLANES, LANES))

    def combine(t, src_a, src_b):
        acc_hi = [jnp.zeros((SUBLANES, LANES), F32) for _ in range(N_VACC)]
        acc_lo = [jnp.zeros((SUBLANES, LANES), F32) for _ in range(N_VACC)]
        prods = []
        for n, (k, row) in enumerate(_slot_rows(tile_ref, t)):
            wv = jnp.where(low_half, jnp.broadcast_to(src_a[k:k + 1, :], (SUBLANES, LANES)),
                           jnp.broadcast_to(src_b[k:k + 1, :], (SUBLANES, LANES)))
            prods.append(pltpu.bitcast(wv, BF16) * pltpu.bitcast(_load_tile(tbl_ref, row), BF16))
            if len(prods) == BF16_GROUP:
                hi, lo = _unpack_tile(pltpu.bitcast((prods[0] + prods[1]) + (prods[2] + prods[3]), I32))
                a = (n // BF16_GROUP) % N_VACC
                acc_hi[a] = acc_hi[a] + hi
                acc_lo[a] = acc_lo[a] + lo
                prods = []
        hi = (acc_hi[0] + acc_hi[1]) + (acc_hi[2] + acc_hi[3])
        lo = (acc_lo[0] + acc_lo[1]) + (acc_lo[2] + acc_lo[3])
        hi = hi + pltpu.roll(hi, HALF_ROWS, 0)
        lo = lo + pltpu.roll(lo, HALF_ROWS, 0)
        out_ref[t] = jnp.where(low_half, lo, hi)

    def token_pair(p, carry):
        t0 = 2 * p
        splat_weights(t0 + 1, wa_odd, wb_odd)
        combine(t0, wa_even, wb_even)
        splat_weights(jnp.minimum(t0 + 2, TB_PEER - 1), wa_even, wb_even)
        combine(t0 + 1, wa_odd, wb_odd)
        return carry

    splat_weights(0, wa_even, wb_even)
    lax.fori_loop(0, TB_PEER // 2, token_pair, 0)


def _peer_combine(tile_tok, wa_t, wb_t, table):
    col = pl.BlockSpec((N_SLOTS, TB_PEER), lambda i: (0, i))
    return pl.pallas_call(
        _peer_combine_kernel,
        grid=(N_TOK // TB_PEER,),
        in_specs=[
            pl.BlockSpec((TB_PEER, N_SLOTS // 2), lambda i: (i, 0), memory_space=pltpu.SMEM),
            col, col,
            pl.BlockSpec((N_TILES * SUBLANES, LANES), lambda i: (0, 0), pipeline_mode=pl.Buffered(1)),
        ],
        out_specs=pl.BlockSpec((TB_PEER, SUBLANES, LANES), lambda i: (i, 0, 0)),
        out_shape=jax.ShapeDtypeStruct((N_TOK, SUBLANES, LANES), F32),
        scratch_shapes=[pltpu.VMEM((N_SLOTS, LANES), F32) for _ in range(4)],
        compiler_params=_cparams(1),
        name="peer_combine",
    )(tile_tok, wa_t, wb_t, table)


def _peer(x2, wq_bf, keys_bf, u_packed, v_packed):
    idx_t, gate_t, rows_t = _peer_route(x2, wq_bf, keys_bf)
    tile_tok = rows_t.T
    da, db = _peer_dots(tile_tok, x2.reshape(N_TOK, SUBLANES, LANES), u_packed)
    wa_t, wb_t = _peer_act(da, db, idx_t, gate_t)
    out3 = _peer_combine(tile_tok, wa_t, wb_t, v_packed)
    return out3.reshape(N_TOK, D_MODEL)


TM_SGU = 256


def _sgu_kernel(h1_ref, peer_ref, mod0_ref, mod_ref, g1_ref, g2_ref, win_ref, lng_ref, lnb_ref, ws_ref, bs_ref,
                wout_ref, h3_ref, x2_ref):
    h2 = h1_ref[...] + mod0_ref[0, 5:6, :] * peer_ref[...]
    hm = _rms_modulate(h2, g1_ref[...], mod_ref[0, 0:1, :], mod_ref[0, 1:2, :])
    z = _gelu_exact(jnp.dot(hm.astype(BF16), win_ref[...], preferred_element_type=F32))
    u = z[:, :D_SGU]
    vv = z[:, D_SGU:]
    mu = jnp.mean(vv, axis=-1, keepdims=True)
    d = vv - mu
    var = jnp.mean(d * d, axis=-1, keepdims=True)
    vn = (d * lax.rsqrt(var + EPS) * lng_ref[...] + lnb_ref[...]).astype(BF16)
    gw = D_SGU // SGU_GROUPS
    rows = []
    for c in range(TM_SGU // CHUNK):
        cols = []
        for g in range(SGU_GROUPS):
            blk = vn[c * CHUNK:(c + 1) * CHUNK, g * gw:(g + 1) * gw]
            cols.append(jnp.dot(ws_ref[g], blk, preferred_element_type=F32) + bs_ref[g])
        rows.append(jnp.concatenate(cols, axis=1))
    mixed = jnp.concatenate(rows, axis=0)
    mix = jnp.dot((u * mixed).astype(BF16), wout_ref[...], preferred_element_type=F32)
    h3 = h2 + mod_ref[0, 2:3, :] * mix
    h3_ref[...] = h3
    x2_ref[...] = _rms_modulate(h3, g2_ref[...], mod_ref[0, 3:4, :], mod_ref[0, 4:5, :])


def _sgu_layer(h1, peer0, mod0, mod1, g1, g2, win_bf, lng, lnb, ws_bf, bs, wout_bf):
    tiles_per_b = SEQ // TM_SGU
    tok = lambda i: (i, 0)
    full = lambda i: (0, 0)
    modspec = pl.BlockSpec((1, N_MOD, D_MODEL), lambda i: (i // tiles_per_b, 0, 0))
    vec = pl.BlockSpec((1, D_MODEL), full)
    return pl.pallas_call(
        _sgu_kernel,
        grid=(N_TOK // TM_SGU,),
        in_specs=[
            pl.BlockSpec((TM_SGU, D_MODEL), tok),
            pl.BlockSpec((TM_SGU, D_MODEL), tok),
            modspec, modspec, vec, vec,
            pl.BlockSpec((D_MODEL, 2 * D_SGU), full),
            vec, vec,
            pl.BlockSpec((SGU_GROUPS, CHUNK, CHUNK), lambda i: (0, 0, 0)),
            pl.BlockSpec((SGU_GROUPS, CHUNK, 1), lambda i: (0, 0, 0)),
            pl.BlockSpec((D_SGU, D_MODEL), full),
        ],
        out_specs=[pl.BlockSpec((TM_SGU, D_MODEL), tok), pl.BlockSpec((TM_SGU, D_MODEL), tok)],
        out_shape=[jax.ShapeDtypeStruct((N_TOK, D_MODEL), F32), jax.ShapeDtypeStruct((N_TOK, D_MODEL), F32)],
        compiler_params=_cparams(1),
        name="sgu_layer",
    )(h1, peer0, mod0, mod1, g1, g2, win_bf, lng, lnb, ws_bf, bs, wout_bf)


TM_FINAL = 512


def _final_kernel(h_ref, peer_ref, mod_ref, g_ref, o_ref):
    h = h_ref[...] + mod_ref[0, 5:6, :] * peer_ref[...]
    o_ref[...] = (h * lax.rsqrt(jnp.mean(h * h, axis=-1, keepdims=True) + EPS)) * g_ref[...]


def _final(h3, peer1, mod1, gf):
    tiles_per_b = SEQ // TM_FINAL
    tok = lambda i: (i, 0)
    return pl.pallas_call(
        _final_kernel,
        grid=(N_TOK // TM_FINAL,),
        in_specs=[
            pl.BlockSpec((TM_FINAL, D_MODEL), tok),
            pl.BlockSpec((TM_FINAL, D_MODEL), tok),
            pl.BlockSpec((1, N_MOD, D_MODEL), lambda i: (i // tiles_per_b, 0, 0)),
            pl.BlockSpec((1, D_MODEL), lambda i: (0, 0)),
        ],
        out_specs=pl.BlockSpec((TM_FINAL, D_MODEL), tok),
        out_shape=jax.ShapeDtypeStruct((N_TOK, D_MODEL), F32),
        compiler_params=_cparams(1),
        name="final_norm",
    )(h3, peer1, mod1, gf)


def kernel(x, c, ctx, c_ctx, norm1_g, norm2_g, ada_w, ada_b, ab_w_in, conv_w, conv_b, conv_gn_g, conv_gn_b,
           na_rpb, ab_w_out, sgu_w_in, sgu_ln_g, sgu_ln_b, sgu_w_s, sgu_b_s, sgu_w_out, peer_wq, peer_keys,
           peer_u, peer_v, norm_f_g):
    assert x.shape == (BATCH, SEQ, D_MODEL) and ctx.shape == (BATCH, CTX_LEN, D_MODEL)
    x2d = x.reshape(N_TOK, D_MODEL)
    row = lambda a: a.reshape(1, -1)

    cc = jnp.concatenate([c, c_ctx[None], jnp.zeros((16 - BATCH - 1, D_MODEL), F32)], axis=0)
    mod_all = _ada_modulation(cc, ada_w, ada_b)
    mod0 = mod_all[0, :BATCH].reshape(BATCH, N_MOD, D_MODEL)
    mod0_c = mod_all[0, BATCH].reshape(N_MOD, D_MODEL)
    mod1 = mod_all[1, :BATCH].reshape(BATCH, N_MOD, D_MODEL)

    w_in = ab_w_in[0].astype(BF16)
    hglu, q, k, v = _inproj0(x2d, mod0, row(norm1_g[0]), w_in)
    kc, vc = _ctx_kv(ctx.reshape(BATCH * CTX_LEN, D_MODEL), mod0_c, row(norm1_g[0]), w_in[:, 2 * D_CONV + D_ATTN:])
    y_a = _conv_module(hglu.reshape(BATCH, SEQ, D_CONV), conv_w[0], conv_b[0], conv_gn_g[0], conv_gn_b[0])
    y_b = _neighbourhood_attention(
        q, k.reshape(BATCH, SEQ, D_ATTN), v.reshape(BATCH, SEQ, D_ATTN),
        kc.reshape(BATCH, CTX_LEN, D_ATTN), vc.reshape(BATCH, CTX_LEN, D_ATTN), _na_bias_table(na_rpb[0]))
    w_out = ab_w_out[0].astype(BF16)
    h1, x2 = _outproj0(y_a.reshape(N_TOK, D_CONV), y_b, x2d, mod0, row(norm2_g[0]), w_out[:D_CONV], w_out[D_CONV:])
    peer0 = _peer(x2, peer_wq[0].astype(BF16), peer_keys[0].astype(BF16),
                  _pack_expert_table(peer_u[0]), _pack_expert_table(peer_v[0]))

    h3, x2b = _sgu_layer(
        h1, peer0, mod0, mod1, row(norm1_g[1]), row(norm2_g[1]), sgu_w_in[0].astype(BF16), row(sgu_ln_g[0]),
        row(sgu_ln_b[0]), sgu_w_s[0].astype(BF16), sgu_b_s[0].reshape(SGU_GROUPS, CHUNK, 1),
        sgu_w_out[0].astype(BF16))
    peer1 = _peer(x2b, peer_wq[1].astype(BF16), peer_keys[1].astype(BF16),
                  _pack_expert_table(peer_u[1]), _pack_expert_table(peer_v[1]))

    out = _final(h3, peer1, mod1, row(norm_f_g))
    return out.reshape(BATCH, SEQ, D_MODEL)
```
